```python
import numpy as np
import jax
import jax.numpy as jnp
from jax import lax

D_MODEL = 1024
BATCH = 2
SEQ = 8192
DEPTH = 2

GRID_W = 64
CTX_LEN = 256
MLSTM_HEADS = 4
MLSTM_HEAD_DIM = D_MODEL // MLSTM_HEADS
D_MLSTM = MLSTM_HEADS * MLSTM_HEAD_DIM
MLSTM_CHUNK = 64
QK_CONV = 5
N_GATES = 2 * 2 * MLSTM_HEADS
FOURIER_GROUPS = 4
D_FOURIER = D_MODEL
FOURIER_GROUP_DIM = D_FOURIER // FOURIER_GROUPS
D_CONV = D_MODEL
DW_KERNEL = 31
D_FF = 2816
N_BRANCH = 3
D_BRANCH = D_MODEL
ALPHA = (2 * DEPTH) ** 0.25
BETA = (8 * DEPTH) ** -0.25
LN_EPS = 1e-5
IN_SIZES = (D_MLSTM, D_MLSTM, D_MLSTM, N_GATES, D_MLSTM, D_FOURIER, D_CONV, D_CONV, N_BRANCH * D_MODEL)
N_IN = sum(IN_SIZES)
N_QKVG = 3 * D_MLSTM + N_GATES

kernel_name = "hybrid_mlstm_fnet_conformer_prefix_dit"


def layer_norm(x, g, b):
    xf = x.astype(jnp.float32)
    mu = jnp.mean(xf, axis=-1, keepdims=True)
    var = jnp.mean(jnp.square(xf - mu), axis=-1, keepdims=True)
    return ((xf - mu) * lax.rsqrt(var + LN_EPS) * g + b).astype(x.dtype)


def head_norm(h, g):
    mu = jnp.mean(h, axis=-1, keepdims=True)
    var = jnp.mean(jnp.square(h - mu), axis=-1, keepdims=True)
    gain = g.astype(jnp.float32).reshape(MLSTM_HEADS, MLSTM_HEAD_DIM)[None, :, None, :]
    return (h - mu) * lax.rsqrt(var + LN_EPS) * gain


def sincos_2d(rows, cols, d):
    quarter = d // 4
    omega = 1.0 / (10000.0 ** (jnp.arange(quarter, dtype=jnp.float32) / quarter))
    r = jnp.arange(rows, dtype=jnp.float32)[:, None] * omega
    cl = jnp.arange(cols, dtype=jnp.float32)[:, None] * omega
    row_emb = jnp.concatenate([jnp.sin(r), jnp.cos(r)], axis=-1)
    col_emb = jnp.concatenate([jnp.sin(cl), jnp.cos(cl)], axis=-1)
    emb = jnp.concatenate([
        jnp.broadcast_to(row_emb[:, None, :], (rows, cols, d // 2)),
        jnp.broadcast_to(col_emb[None, :, :], (rows, cols, d // 2))], axis=-1)
    return emb.reshape(rows * cols, d)


def depthwise_conv(x, w):
    k = w.shape[0]
    return lax.conv_general_dilated(
        x, w[:, None, :], window_strides=(1,), padding=[(k // 2, k // 2)],
        dimension_numbers=('NWC', 'WIO', 'NWC'), feature_group_count=x.shape[-1])


def swiglu(u, w1, w3, w2):
    return (jax.nn.silu(u @ w1) * (u @ w3)) @ w2


def mlstm_scan(q, k, v, log_i, log_f, state, with_output):
    bsz, nh, t, dh = q.shape
    nc = t // MLSTM_CHUNK

    def chunks(a):
        a = a.reshape((bsz, nh, nc, MLSTM_CHUNK) + a.shape[3:])
        return jnp.moveaxis(a, 2, 0)

    xs = (chunks(q), chunks(k), chunks(v), chunks(log_i), chunks(log_f))
    scan_order = jnp.tril(jnp.ones((MLSTM_CHUNK, MLSTM_CHUNK), dtype=bool))

    def step(carry, inp):
        c_mat, n_vec, m = carry
        qc, kc, vc, li, lf = inp
        b = jnp.cumsum(lf, axis=-1)
        b_end = b[..., -1]
        g = b_end[..., None] - b + li
        m_new = jnp.maximum(b_end + m, jnp.max(g, axis=-1))
        carry_w = jnp.exp(b_end + m - m_new)
        tok_w = jnp.exp(g - m_new[..., None])
        c_new = carry_w[..., None, None] * c_mat + jnp.einsum('bhsv,bhsk->bhvk', vc * tok_w[..., None], kc)
        n_new = carry_w[..., None] * n_vec + jnp.einsum('bhs,bhsk->bhk', tok_w, kc)
        new = (c_new, n_new, m_new)
        if not with_output:
            return new, None
        d_log = jnp.where(scan_order, b[..., :, None] - b[..., None, :] + li[..., None, :], -jnp.inf)
        inter = b + m[..., None]
        m_t = jnp.maximum(inter, jnp.max(d_log, axis=-1))
        scores = jnp.einsum('bhtk,bhsk->bhts', qc, kc) * jnp.exp(d_log - m_t[..., None])
        inter_w = jnp.exp(inter - m_t)
        num = (jnp.einsum('bhts,bhsv->bhtv', scores, vc)
               + inter_w[..., None] * jnp.einsum('bhvk,bhtk->bhtv', c_mat, qc))
        den = jnp.sum(scores, axis=-1) + inter_w * jnp.einsum('bhk,bhtk->bht', n_vec, qc)
        h = num / jnp.maximum(jnp.abs(den), jnp.exp(-m_t))[..., None]
        return new, h

    state, hs = lax.scan(step, state, xs)
    if not with_output:
        return state, None
    return state, jnp.moveaxis(hs, 0, 2).reshape(bsz, nh, t, dh)


def mlstm_branch(q, k, v, gate_pre, b_gates, qk_w, init_states, with_output):
    bsz, t, _ = q.shape
    q, k = jnp.split(depthwise_conv(jnp.concatenate([q, k], axis=-1), qk_w), 2, axis=-1)

    def heads(a):
        return a.astype(jnp.float32).reshape(bsz, t, MLSTM_HEADS, MLSTM_HEAD_DIM).transpose(0, 2, 1, 3)

    q, k, v = heads(q), heads(k) * MLSTM_HEAD_DIM ** -0.5, heads(v)
    g = (gate_pre + b_gates).astype(jnp.float32).reshape(bsz, t, 2, 2, MLSTM_HEADS).transpose(2, 3, 0, 4, 1)
    log_i = g[:, 0]
    log_f = jax.nn.log_sigmoid(g[:, 1])
    st_f, h_f = mlstm_scan(q, k, v, log_i[0], log_f[0], init_states[0], with_output)
    rev = lambda a: jnp.flip(a, axis=2)
    st_b, h_b = mlstm_scan(rev(q), rev(k), rev(v), rev(log_i[1]), rev(log_f[1]), init_states[1], with_output)
    if not with_output:
        return None, (st_f, st_b)
    return h_f + rev(h_b), (st_f, st_b)


def context_mlstm_states(u, w_in, b_gates, qk_w, init_states):
    q, k, v, gp = jnp.split(u @ w_in[:, :N_QKVG], [D_MLSTM, 2 * D_MLSTM, 3 * D_MLSTM], axis=-1)
    _, states = mlstm_branch(q, k, v, gp, b_gates, qk_w, init_states, False)
    return states


def mixer(u, w_in, b_gates, qk_w, mh_g, dw_w, dw_b, cn_g, cn_b, w_branch, w_out, init_states):
    bsz, t, _ = u.shape
    split_at = np.cumsum(IN_SIZES)[:-1].tolist()
    q, k, v, gp, o, f_in, c_val, c_gate, m_gate = jnp.split(u @ w_in, split_at, axis=-1)
    h, states = mlstm_branch(q, k, v, gp, b_gates, qk_w, init_states, True)
    h_a = head_norm(h, mh_g).transpose(0, 2, 1, 3).reshape(bsz, t, D_MLSTM).astype(u.dtype) * jax.nn.sigmoid(o)
    f = f_in.astype(jnp.float32).reshape(bsz, t, FOURIER_GROUPS, FOURIER_GROUP_DIM)
    h_b = jnp.fft.fftn(f, axes=(1, 3), norm='ortho').real.reshape(bsz, t, D_FOURIER).astype(u.dtype)
    cv = depthwise_conv(c_val * jax.nn.sigmoid(c_gate), dw_w) + dw_b
    h_c = jax.nn.silu(layer_norm(cv, cn_g, cn_b))
    gates = jax.nn.sigmoid(m_gate).reshape(bsz, t, N_BRANCH, D_MODEL)
    merged = (gates[:, :, 0] * (h_a @ w_branch[0])
              + gates[:, :, 1] * (h_b @ w_branch[1])
              + gates[:, :, 2] * (h_c @ w_branch[2]))
    return merged @ w_out, states


def ffn_sublayer(h, mod, w1, w3, w2, g, b):
    u = h * (1 + mod[1]) + mod[0]
    return layer_norm(ALPHA * h + 0.5 * mod[2] * swiglu(u, w1, w3, w2), g, b)


def setup_inputs(seed: int = 0) -> dict:
    key = jax.random.key(seed)
    ks = jax.random.split(key, 24)
    nrm = lambda k, s: jax.random.normal(k, s, dtype=jnp.float32)
    i_bias = 0.1 * nrm(ks[10], (DEPTH, 2, 1, MLSTM_HEADS))
    f_bias = jnp.linspace(3.0, 6.0, MLSTM_HEADS, dtype=jnp.float32) + 0.1 * nrm(ks[11], (DEPTH, 2, 1, MLSTM_HEADS))
    return {
        "x": nrm(ks[0], (BATCH, SEQ, D_MODEL)),
        "c": nrm(ks[1], (BATCH, D_MODEL)),
        "ctx": nrm(ks[2], (BATCH, CTX_LEN, D_MODEL)),
        "c_ctx": nrm(ks[3], (D_MODEL,)),
        "w_ada": nrm(ks[4], (DEPTH, D_MODEL, 9 * D_MODEL)) * (0.5 * D_MODEL ** -0.5),
        "b_ada": 0.02 * nrm(ks[5], (DEPTH, 9 * D_MODEL)),
        "ln_g": 1.0 + 0.02 * nrm(ks[6], (DEPTH, 3, D_MODEL)),
        "ln_b": 0.02 * nrm(ks[7], (DEPTH, 3, D_MODEL)),
        "ffn_w1": nrm(ks[8], (DEPTH, 2, D_MODEL, D_FF)) * D_MODEL ** -0.5,
        "ffn_w3": nrm(ks[9], (DEPTH, 2, D_MODEL, D_FF)) * D_MODEL ** -0.5,
        "ffn_w2": nrm(ks[12], (DEPTH, 2, D_FF, D_MODEL)) * (D_FF ** -0.5 * BETA),
        "w_in": nrm(ks[13], (DEPTH, D_MODEL, N_IN)) * D_MODEL ** -0.5,
        "b_gates": jnp.concatenate([i_bias, f_bias], axis=2).reshape(DEPTH, N_GATES),
        "qk_conv_w": nrm(ks[14], (DEPTH, QK_CONV, 2 * D_MLSTM)) * QK_CONV ** -0.5,
        "mh_norm_g": 1.0 + 0.02 * nrm(ks[15], (DEPTH, D_MLSTM)),
        "dw_w": nrm(ks[16], (DEPTH, DW_KERNEL, D_CONV)) * DW_KERNEL ** -0.5,
        "dw_b": 0.02 * nrm(ks[17], (DEPTH, D_CONV)),
        "conv_norm_g": 1.0 + 0.02 * nrm(ks[18], (DEPTH, D_CONV)),
        "conv_norm_b": 0.02 * nrm(ks[19], (DEPTH, D_CONV)),
        "w_branch": nrm(ks[20], (DEPTH, N_BRANCH, D_BRANCH, D_MODEL)) * D_BRANCH ** -0.5,
        "w_out": nrm(ks[21], (DEPTH, D_MODEL, D_MODEL)) * (D_MODEL ** -0.5 * BETA),
    }


def reference(x, c, ctx, c_ctx, w_ada, b_ada, ln_g, ln_b, ffn_w1, ffn_w3, ffn_w2, w_in, b_gates,
              qk_conv_w, mh_norm_g, dw_w, dw_b, conv_norm_g, conv_norm_b, w_branch, w_out):
    bsz, t, d = x.shape
    rows = t // GRID_W
    x = x + sincos_2d(rows, GRID_W, d).astype(x.dtype)[None]
    xc = ctx
    zero_state = (jnp.zeros((bsz, MLSTM_HEADS, MLSTM_HEAD_DIM, MLSTM_HEAD_DIM), jnp.float32),
                  jnp.zeros((bsz, MLSTM_HEADS, MLSTM_HEAD_DIM), jnp.float32),
                  jnp.zeros((bsz, MLSTM_HEADS), jnp.float32))
    init = (zero_state, zero_state)
    for l in range(DEPTH):
        last = l == DEPTH - 1
        ada_x = (jax.nn.silu(c) @ w_ada[l] + b_ada[l]).reshape(bsz, 3, 3, d).transpose(1, 2, 0, 3)[:, :, :, None, :]
        ada_c = (jax.nn.silu(c_ctx) @ w_ada[l] + b_ada[l]).reshape(3, 3, d)
        mix_args = (w_in[l], b_gates[l], qk_conv_w[l], mh_norm_g[l], dw_w[l], dw_b[l],
                    conv_norm_g[l], conv_norm_b[l], w_branch[l], w_out[l])
        x = ffn_sublayer(x, ada_x[0], ffn_w1[l, 0], ffn_w3[l, 0], ffn_w2[l, 0], ln_g[l, 0], ln_b[l, 0])
        xc = ffn_sublayer(xc, ada_c[0], ffn_w1[l, 0], ffn_w3[l, 0], ffn_w2[l, 0], ln_g[l, 0], ln_b[l, 0])
        uc = xc * (1 + ada_c[1, 1]) + ada_c[1, 0]
        if last:
            ctx_states = context_mlstm_states(uc, w_in[l], b_gates[l], qk_conv_w[l], init)
        else:
            y_c, ctx_states = mixer(uc, *mix_args, init)
            xc = layer_norm(ALPHA * xc + ada_c[1, 2] * y_c, ln_g[l, 1], ln_b[l, 1])
        ux = x * (1 + ada_x[1, 1]) + ada_x[1, 0]
        y_x, _ = mixer(ux, *mix_args, ctx_states)
        x = layer_norm(ALPHA * x + ada_x[1, 2] * y_x, ln_g[l, 1], ln_b[l, 1])
        x = ffn_sublayer(x, ada_x[2], ffn_w1[l, 1], ffn_w3[l, 1], ffn_w2[l, 1], ln_g[l, 2], ln_b[l, 2])
        if not last:
            xc = ffn_sublayer(xc, ada_c[2], ffn_w1[l, 1], ffn_w3[l, 1], ffn_w2[l, 1], ln_g[l, 2], ln_b[l, 2])
    return x
```

```python
import functools

import numpy as np
import jax
import jax.numpy as jnp
from jax import lax
from jax.experimental import pallas as pl
from jax.experimental.pallas import tpu as pltpu

F32 = jnp.float32
BF16 = jnp.bfloat16

GRID_W = 64
HEADS = 4
N_GATES = 16
FOURIER_GROUPS = 4
N_BRANCH = 3
LN_EPS = 1e-5

V7X_VMEM_BYTES = 64 * 1024 * 1024
V7X_LANES = 128
SUBLANES = 8
V7X_BF16_SUBLANES = 16

MLSTM_CHUNK = 256
HALO = V7X_BF16_SUBLANES


def _vmem_limit(nbytes):
    return int(min(max(nbytes, 16 * 1024 * 1024), V7X_VMEM_BYTES - 6 * 1024 * 1024))


def _params(semantics, nbytes):
    return pltpu.CompilerParams(dimension_semantics=semantics, vmem_limit_bytes=_vmem_limit(nbytes))


def _dot(a, b):
    return jnp.dot(a, b, preferred_element_type=F32)


def _layer_norm(z, g, b):
    mu = jnp.mean(z, axis=-1, keepdims=True)
    d = z - mu
    var = jnp.mean(d * d, axis=-1, keepdims=True)
    return d * lax.rsqrt(var + LN_EPS) * g + b


def _sigmoid(x):
    return 1.0 / (1.0 + jnp.exp(-x))


def _ada_kernel(c_ref, w_ref, b_ref, o_ref):
    c = c_ref[...]
    a = (c * _sigmoid(c)).astype(BF16)
    o_ref[0] = _dot(a, w_ref[0].astype(BF16)) + b_ref[0]


def _ada(c_rows, w_ada, b_ada):
    depth, d, n = w_ada.shape
    tn = 1024
    return pl.pallas_call(
        _ada_kernel,
        grid=(depth, n // tn),
        in_specs=[
            pl.BlockSpec((8, d), lambda l, j: (0, 0)),
            pl.BlockSpec((1, d, tn), lambda l, j: (l, 0, j)),
            pl.BlockSpec((1, 1, tn), lambda l, j: (l, 0, j)),
        ],
        out_specs=pl.BlockSpec((1, 8, tn), lambda l, j: (l, 0, j)),
        out_shape=jax.ShapeDtypeStruct((depth, 8, n), F32),
        compiler_params=_params(("parallel", "parallel"), 4 * d * tn * 4),
        name="ada",
    )(c_rows, w_ada, b_ada.reshape(depth, 1, n))


def _embed_kernel(x_ref, r_ref, c_ref, o_ref):
    half = r_ref.shape[-1]
    o_ref[0, :, :, :half] = x_ref[0, :, :, :half] + r_ref[...]
    o_ref[0, :, :, half:] = x_ref[0, :, :, half:] + c_ref[...][None]


def _embed(x):
    bsz, t, d = x.shape
    rows = t // GRID_W
    quarter = d // 4
    omega = 1.0 / (10000.0 ** (jnp.arange(quarter, dtype=F32) / quarter))
    r = jnp.arange(rows, dtype=F32)[:, None] * omega
    cl = jnp.arange(GRID_W, dtype=F32)[:, None] * omega
    row_emb = jnp.concatenate([jnp.sin(r), jnp.cos(r)], axis=-1).reshape(rows, 1, d // 2)
    col_emb = jnp.concatenate([jnp.sin(cl), jnp.cos(cl)], axis=-1)
    rb = 8
    out = pl.pallas_call(
        _embed_kernel,
        grid=(bsz, rows // rb),
        in_specs=[
            pl.BlockSpec((1, rb, GRID_W, d), lambda b, i: (b, i, 0, 0)),
            pl.BlockSpec((rb, 1, d // 2), lambda b, i: (i, 0, 0)),
            pl.BlockSpec((GRID_W, d // 2), lambda b, i: (0, 0)),
        ],
        out_specs=pl.BlockSpec((1, rb, GRID_W, d), lambda b, i: (b, i, 0, 0)),
        out_shape=jax.ShapeDtypeStruct((bsz, rows, GRID_W, d), x.dtype),
        compiler_params=_params(("parallel", "parallel"), 6 * rb * GRID_W * d * 4),
        name="embed",
    )(x.reshape(bsz, rows, GRID_W, d), row_emb, col_emb)
    return out.reshape(bsz, t, d)


def _ffn_chunks(f):
    chunks, c0 = [], 0
    while c0 < f:
        cw = min(1024, f - c0)
        chunks.append((c0, cw))
        c0 += cw
    return tuple(chunks)


def _ffn_kernel(h_ref, mod_ref, modn_ref, w1_ref, w3_ref, w2_ref, g_ref, b_ref, o_ref, *u_ref,
                alpha, chunks):
    h = h_ref[...]
    shift, scale, gate = mod_ref[0, 0:1, :], mod_ref[0, 1:2, :], mod_ref[0, 2:3, :]
    u = (h * (1.0 + scale) + shift).astype(BF16)
    acc = None
    for c0, cw in chunks:
        a1 = _dot(u, w1_ref[:, c0:c0 + cw])
        a3 = _dot(u, w3_ref[:, c0:c0 + cw])
        hid = (a1 * _sigmoid(a1) * a3).astype(BF16)
        part = _dot(hid, w2_ref[c0:c0 + cw, :])
        acc = part if acc is None else acc + part
    y = _layer_norm(alpha * h + 0.5 * gate * acc, g_ref[...], b_ref[...])
    o_ref[...] = y
    if u_ref:
        u_ref[0][...] = (y * (1.0 + modn_ref[0, 1:2, :]) + modn_ref[0, 0:1, :]).astype(BF16)


def _ffn(h, mod, modn, w1, w3, w2, g, b, alpha, emit_u):
    bsz, t, d = h.shape
    f = w1.shape[1]
    tm = min(512, t)
    tpb = t // tm
    n = bsz * t
    row = lambda i: (i, 0)
    modmap = lambda i: (i // tpb, 0, 0)
    const = lambda i: (0, 0)
    out_shape = [jax.ShapeDtypeStruct((n, d), F32)]
    out_specs = [pl.BlockSpec((tm, d), row)]
    if emit_u:
        out_shape.append(jax.ShapeDtypeStruct((n, d), BF16))
        out_specs.append(pl.BlockSpec((tm, d), row))
    wbytes = 3 * d * f * 2
    res = pl.pallas_call(
        functools.partial(_ffn_kernel, alpha=alpha, chunks=_ffn_chunks(f)),
        grid=(n // tm,),
        in_specs=[
            pl.BlockSpec((tm, d), row),
            pl.BlockSpec((1, 3, d), modmap),
            pl.BlockSpec((1, 3, d), modmap),
            pl.BlockSpec((d, f), const, pipeline_mode=pl.Buffered(1)),
            pl.BlockSpec((d, f), const, pipeline_mode=pl.Buffered(1)),
            pl.BlockSpec((f, d), const, pipeline_mode=pl.Buffered(1)),
            pl.BlockSpec((1, d), const),
            pl.BlockSpec((1, d), const),
        ],
        out_specs=out_specs,
        out_shape=out_shape,
        compiler_params=_params(("parallel",), wbytes + 6 * tm * d * 4 + 6 * tm * 1024 * 4),
        name="ffn",
    )(h.reshape(n, d), mod, modn, w1, w3, w2, g.reshape(1, d), b.reshape(1, d))
    y = res[0].reshape(bsz, t, d)
    return (y, res[1].reshape(bsz, t, d)) if emit_u else (y, None)


def _linear_kernel(u_ref, w_ref, b_ref, o_ref):
    o_ref[...] = (_dot(u_ref[...], w_ref[...]) + b_ref[...]).astype(o_ref.dtype)


def _glu_kernel(u_ref, wa_ref, wb_ref, o_ref):
    u = u_ref[...]
    o_ref[...] = (_dot(u, wa_ref[...]) * _sigmoid(_dot(u, wb_ref[...]))).astype(o_ref.dtype)


def _linear(u, w, bias, out_dtype):
    n, k = u.shape
    m = w.shape[1]
    tm = min(1024, n)
    tn = min(1024, m)
    return pl.pallas_call(
        _linear_kernel,
        grid=(n // tm, m // tn),
        in_specs=[
            pl.BlockSpec((tm, k), lambda i, j: (i, 0)),
            pl.BlockSpec((k, tn), lambda i, j: (0, j)),
            pl.BlockSpec((1, tn), lambda i, j: (0, j)),
        ],
        out_specs=pl.BlockSpec((tm, tn), lambda i, j: (i, j)),
        out_shape=jax.ShapeDtypeStruct((n, m), out_dtype),
        compiler_params=_params(("parallel", "arbitrary"), 4 * tm * k + 4 * k * tn + 16 * tm * tn),
        name="linear",
    )(u, w, bias)


def _glu(u, wa, wb, out_dtype):
    n, k = u.shape
    m = wa.shape[1]
    tm = min(1024, n)
    tn = min(1024, m)
    return pl.pallas_call(
        _glu_kernel,
        grid=(n // tm, m // tn),
        in_specs=[
            pl.BlockSpec((tm, k), lambda i, j: (i, 0)),
            pl.BlockSpec((k, tn), lambda i, j: (0, j)),
            pl.BlockSpec((k, tn), lambda i, j: (0, j)),
        ],
        out_specs=pl.BlockSpec((tm, tn), lambda i, j: (i, j)),
        out_shape=jax.ShapeDtypeStruct((n, m), out_dtype),
        compiler_params=_params(("parallel", "arbitrary"), 4 * tm * k + 8 * k * tn + 20 * tm * tn),
        name="glu",
    )(u, wa, wb)


def _dwconv_kernel(xp_ref, xm_ref, xn_ref, w_ref, *rest, ksize, rb, norm):
    if norm:
        bias_ref, g_ref, b_ref, o_ref, win_ref = rest
    else:
        o_ref, win_ref = rest
    tt = xm_ref.shape[1]
    rows = tt + 2 * HALO
    i = pl.program_id(1)
    last = pl.num_programs(1) - 1
    win_ref[0, 0:HALO, :] = jnp.where(i > 0, xp_ref[0].astype(F32), 0.0)
    win_ref[0, HALO:HALO + tt, :] = xm_ref[0].astype(F32)
    win_ref[0, HALO + tt:, :] = jnp.where(i < last, xn_ref[0].astype(F32), 0.0)
    win = win_ref[0]
    for s in range(1, SUBLANES):
        win_ref[s] = pltpu.roll(win, rows - s, 0)
    pad = ksize // 2

    def block(ib, carry):
        r0 = pl.multiple_of(ib * rb, rb)
        acc = None
        for k in range(ksize):
            a, s = divmod(HALO - pad + k, SUBLANES)
            term = win_ref[s, pl.ds(r0 + a * SUBLANES, rb), :] * w_ref[k:k + 1, :]
            acc = term if acc is None else acc + term
        if norm:
            z = _layer_norm(acc + bias_ref[...], g_ref[...], b_ref[...])
            acc = z * _sigmoid(z)
        o_ref[0, pl.ds(r0, rb), :] = acc.astype(o_ref.dtype)
        return carry

    lax.fori_loop(0, tt // rb, block, 0)


def _dwconv(x, ncols, w, out_dtype, norm=None):
    bsz, t, _ = x.shape
    ksize = w.shape[0]
    assert ksize // 2 <= HALO
    tt = min(256, t)
    tc = ncols if norm is not None else min(512, ncols)
    nh = t // HALO
    per = tt // HALO
    in_specs = [
        pl.BlockSpec((1, HALO, tc), lambda b, i, j: (b, jnp.maximum(i * per - 1, 0), j)),
        pl.BlockSpec((1, tt, tc), lambda b, i, j: (b, i, j)),
        pl.BlockSpec((1, HALO, tc), lambda b, i, j: (b, jnp.minimum((i + 1) * per, nh - 1), j)),
        pl.BlockSpec((ksize, tc), lambda b, i, j: (0, j)),
    ]
    args = [x, x, x, w]
    if norm is not None:
        in_specs += [pl.BlockSpec((1, tc), lambda b, i, j: (0, j))] * 3
        args += [a.reshape(1, ncols) for a in norm]
    return pl.pallas_call(
        functools.partial(_dwconv_kernel, ksize=ksize, rb=16, norm=norm is not None),
        grid=(bsz, t // tt, ncols // tc),
        in_specs=in_specs,
        out_specs=pl.BlockSpec((1, tt, tc), lambda b, i, j: (b, i, j)),
        out_shape=jax.ShapeDtypeStruct((bsz, t, ncols), out_dtype),
        scratch_shapes=[pltpu.VMEM((SUBLANES, tt + 2 * HALO, tc), F32)],
        compiler_params=_params(("parallel", "parallel", "parallel"), 16 * (tt + 2 * HALO) * tc * 4),
        name="dwconv",
    )(*args)


def _split3(x):
    hi = x.astype(BF16)
    r1 = x - hi.astype(F32)
    mid = r1.astype(BF16)
    lo = (r1 - mid.astype(F32)).astype(BF16)
    return hi, mid, lo


def _log_sigmoid(x):
    return jnp.minimum(x, 0.0) - jnp.log(1.0 + jnp.exp(-jnp.abs(x)))


def _mlstm_kernel(q_ref, k_ref, v_ref, gc_ref, gr_ref, ct0_ref, n0_ref, m0_ref, *rest,
                  reverse, finalize):
    if finalize:
        hf_ref, og_ref, mg_ref, h_ref, ctf_ref, nf_ref, mf_ref, ct_s, n_s, m_s = rest
    else:
        h_ref, ctf_ref, nf_ref, mf_ref, ct_s, n_s, m_s = rest
    c = pl.program_id(2)
    ln = q_ref.shape[1]

    @pl.when(c == 0)
    def _():
        ct_s[...] = ct0_ref[0, 0]
        n_s[...] = n0_ref[0, 0]
        m_s[...] = m0_ref[0, 0]

    q = q_ref[0]
    k = k_ref[0]
    v = v_ref[0]
    gc = gc_ref[0, 0]
    gr = gr_ref[0, 0]
    li_col, lf_col = gc[:, 0:1], _log_sigmoid(gc[:, 1:2])
    li_row, lf_row = gr[0:1, :], _log_sigmoid(gr[1:2, :])

    t_idx = lax.broadcasted_iota(jnp.int32, (ln, ln), 0)
    s_idx = lax.broadcasted_iota(jnp.int32, (ln, ln), 1)
    keep = (s_idx >= t_idx) if reverse else (s_idx <= t_idx)
    tri = keep.astype(BF16)
    b_col = sum(_dot(tri, jnp.broadcast_to(p, (ln, V7X_LANES))) for p in _split3(lf_col))[:, 0:1]
    b_row = sum(lax.dot_general(jnp.broadcast_to(p, (8, ln)), tri, (((1,), (1,)), ((), ())),
                                preferred_element_type=F32) for p in _split3(lf_row))[0:1, :]
    b_end = jnp.sum(lf_row, axis=1, keepdims=True)

    m_prev = m_s[0:1, 0:1]
    d_log = jnp.where(keep, b_col - b_row + li_row, -jnp.inf)
    m_t = jnp.maximum(b_col + m_prev, jnp.max(d_log, axis=1, keepdims=True))
    scores = lax.dot_general(q, k, (((1,), (1,)), ((), ())), preferred_element_type=F32)
    scores = scores * jnp.exp(d_log - m_t)
    inter_w = jnp.exp(b_col + m_prev - m_t)
    ct = ct_s[...]
    num = _dot(scores.astype(BF16), v) + inter_w * _dot(q, ct.astype(BF16))
    qn = jnp.sum(q.astype(F32) * n_s[...], axis=1, keepdims=True)
    den = jnp.sum(scores, axis=1, keepdims=True) + inter_w * qn
    h = num / jnp.maximum(jnp.abs(den), jnp.exp(-m_t))

    g_row = b_end - b_row + li_row
    m_new = jnp.maximum(b_end + m_prev, jnp.max(g_row, axis=1, keepdims=True))
    carry = jnp.exp(b_end + m_prev - m_new)
    tok_col = jnp.exp(b_end - b_col + li_col - m_new)
    vw = (v.astype(F32) * tok_col).astype(BF16)
    ct_s[...] = carry * ct + lax.dot_general(k, vw, (((0,), (0,)), ((), ())),
                                             preferred_element_type=F32)
    n_s[...] = carry * n_s[...] + jnp.sum(k.astype(F32) * tok_col, axis=0, keepdims=True)
    m_s[...] = jnp.broadcast_to(m_new, m_s.shape)

    if finalize:
        hs = hf_ref[0] + h
        mu = jnp.mean(hs, axis=-1, keepdims=True)
        dlt = hs - mu
        var = jnp.mean(dlt * dlt, axis=-1, keepdims=True)
        hn = dlt * lax.rsqrt(var + LN_EPS) * mg_ref[...]
        h_ref[0] = (hn * _sigmoid(og_ref[0].astype(F32))).astype(h_ref.dtype)
    else:
        h_ref[0] = h

    @pl.when(c == pl.num_programs(2) - 1)
    def _():
        ctf_ref[0, 0] = ct_s[...]
        nf_ref[0, 0] = n_s[...]
        mf_ref[0, 0] = m_s[...]


def _mlstm_dir(qk, p, gcol, grow, state, reverse, fin):
    bsz, t, dm2 = qk.shape
    dm = dm2 // 2
    dh = dm // HEADS
    ln = min(MLSTM_CHUNK, t)
    nc = t // ln
    di = 1 if reverse else 0
    cm = (lambda c: nc - 1 - c) if reverse else (lambda c: c)
    blk = lambda off: pl.BlockSpec((1, ln, dh), lambda b, h, c: (b, cm(c), off + h))
    st4 = lambda s: pl.BlockSpec((1, 1) + s, lambda b, h, c: (b, h, 0, 0))
    in_specs = [
        blk(0), blk(HEADS),
        pl.BlockSpec((1, ln, dh), lambda b, h, c: (b, cm(c), 2 * HEADS + h)),
        pl.BlockSpec((1, 1, ln, 2), lambda b, h, c: (b, di * HEADS + h, cm(c), 0)),
        pl.BlockSpec((1, 1, 2, ln), lambda b, h, c: (b, di * HEADS + h, 0, cm(c))),
        st4((dh, dh)), st4((1, dh)), st4((8, V7X_LANES)),
    ]
    args = [qk, qk, p, gcol, grow, *state]
    if fin is not None:
        h_fwd, mh_g = fin
        in_specs += [
            pl.BlockSpec((1, ln, dh), lambda b, h, c: (b, cm(c), h)),
            pl.BlockSpec((1, ln, dh), lambda b, h, c: (b, cm(c), 3 * HEADS + h)),
            pl.BlockSpec((1, dh), lambda b, h, c: (0, h)),
        ]
        args += [h_fwd, p, mh_g.reshape(1, dm)]
    out_dtype = BF16 if fin is not None else F32
    outs = pl.pallas_call(
        functools.partial(_mlstm_kernel, reverse=reverse, finalize=fin is not None),
        grid=(bsz, HEADS, nc),
        in_specs=in_specs,
        out_specs=[
            pl.BlockSpec((1, ln, dh), lambda b, h, c: (b, cm(c), h)),
            st4((dh, dh)), st4((1, dh)), st4((8, V7X_LANES)),
        ],
        out_shape=[
            jax.ShapeDtypeStruct((bsz, t, dm), out_dtype),
            jax.ShapeDtypeStruct((bsz, HEADS, dh, dh), F32),
            jax.ShapeDtypeStruct((bsz, HEADS, 1, dh), F32),
            jax.ShapeDtypeStruct((bsz, HEADS, 8, V7X_LANES), F32),
        ],
        scratch_shapes=[pltpu.VMEM((dh, dh), F32), pltpu.VMEM((1, dh), F32),
                        pltpu.VMEM((8, V7X_LANES), F32)],
        compiler_params=_params(("parallel", "parallel", "arbitrary"), 40 * ln * max(ln, dh) * 4),
        name="mlstm_bwd" if reverse else "mlstm_fwd",
    )(*args)
    return outs[0], tuple(outs[1:])


def _zero_state(bsz, dh):
    return (jnp.zeros((bsz, HEADS, dh, dh), F32), jnp.zeros((bsz, HEADS, 1, dh), F32),
            jnp.zeros((bsz, HEADS, 8, V7X_LANES), F32))


def _mlstm(qk, p, gates, init, mh_g):
    bsz, t, _ = qk.shape
    g = gates[:, :, :N_GATES].reshape(bsz, t, 2, 2, HEADS)
    gcol = g.transpose(0, 2, 4, 1, 3).reshape(bsz, 2 * HEADS, t, 2)
    grow = g.transpose(0, 2, 4, 3, 1).reshape(bsz, 2 * HEADS, 2, t)
    h_f, st_f = _mlstm_dir(qk, p, gcol, grow, init[0], False, None)
    fin = None if mh_g is None else (h_f, mh_g)
    h_a, st_b = _mlstm_dir(qk, p, gcol, grow, init[1], True, fin)
    return h_a, (st_f, st_b)


def _dft_mats(n):
    ang = 2.0 * np.pi * np.outer(np.arange(n), np.arange(n)) / n
    return np.cos(ang), np.sin(ang)


def _dft1_kernel(x_ref, cs_ref, twc_ref, tws_ref, yr_ref, yi_ref, *, d):
    t1 = x_ref.shape[1]
    nb = x_ref.shape[2] // d
    cs = cs_ref[...]
    for jj in range(nb):
        a = _dot(cs, x_ref[0, :, jj * d:(jj + 1) * d])
        zr, zi = a[:t1], -a[t1:]
        ct, st = twc_ref[0, :, jj:jj + 1], tws_ref[0, :, jj:jj + 1]
        yr_ref[0, :, jj * d:(jj + 1) * d] = (zr * ct + zi * st).astype(yr_ref.dtype)
        yi_ref[0, :, jj * d:(jj + 1) * d] = (zi * ct - zr * st).astype(yi_ref.dtype)


def _dft2_kernel(*refs, real_input, groups):
    if real_input:
        yr_ref, cs_ref, cc_ref, sc_ref, o_ref = refs
    else:
        yr_ref, yi_ref, cs_ref, cc_ref, sc_ref, o_ref = refs
    t2 = cs_ref.shape[1]
    cs = cs_ref[...]
    a = _dot(cs, yr_ref[0, 0])
    if real_input:
        vr, vi = a[:t2], -a[t2:]
    else:
        b = _dot(cs, yi_ref[0, 0])
        vr, vi = a[:t2] + b[t2:], b[:t2] - a[t2:]
    gd = vr.shape[1] // groups
    for g in range(groups):
        sl = slice(g * gd, (g + 1) * gd)
        o_ref[0, :, sl] = (_dot(vr[:, sl].astype(BF16), cc_ref[...])
                           + _dot(vi[:, sl].astype(BF16), sc_ref[...])).astype(o_ref.dtype)


def _fnet(f):
    bsz, t, d = f.shape
    gd = d // FOURIER_GROUPS
    t2 = min(128, t) if t >= 2048 else t
    t1 = t // t2
    scale = 1.0 / np.sqrt(float(t) * gd)
    cc, sc = _dft_mats(gd)
    cc = jnp.asarray(cc * scale, BF16)
    sc = jnp.asarray(sc * scale, BF16)
    c2, s2 = _dft_mats(t2)
    cs2 = jnp.asarray(np.concatenate([c2, s2], axis=0), BF16)
    const2 = lambda b, k: (0, 0)
    if t1 == 1:
        ys = [f.reshape(bsz, 1, t2, d)]
    else:
        nb = 8
        c1, s1 = _dft_mats(t1)
        cs1 = jnp.asarray(np.concatenate([c1, s1], axis=0), BF16)
        ang = 2.0 * np.pi * np.outer(np.arange(t1), np.arange(t2)) / t
        tw = lambda m: jnp.asarray(m.reshape(t1, t2 // nb, nb).transpose(1, 0, 2), F32)
        blk = pl.BlockSpec((1, t1, nb * d), lambda b, j: (b, 0, j))
        twb = pl.BlockSpec((1, t1, nb), lambda b, j: (j, 0, 0))
        ys = pl.pallas_call(
            functools.partial(_dft1_kernel, d=d),
            grid=(bsz, t2 // nb),
            in_specs=[blk, pl.BlockSpec((2 * t1, t1), lambda b, j: (0, 0)), twb, twb],
            out_specs=[blk, blk],
            out_shape=[jax.ShapeDtypeStruct((bsz, t1, t2 * d), BF16)] * 2,
            compiler_params=_params(("parallel", "parallel"), 16 * t1 * nb * d * 4),
            name="dft1",
        )(f.reshape(bsz, t1, t2 * d), cs1, tw(np.cos(ang)), tw(np.sin(ang)))
        ys = [y.reshape(bsz, t1, t2, d) for y in ys]
    yblk = pl.BlockSpec((1, 1, t2, d), lambda b, k: (b, k, 0, 0))
    out = pl.pallas_call(
        functools.partial(_dft2_kernel, real_input=t1 == 1, groups=FOURIER_GROUPS),
        grid=(bsz, t1),
        in_specs=[yblk] * len(ys) + [
            pl.BlockSpec((2 * t2, t2), const2),
            pl.BlockSpec((gd, gd), const2),
            pl.BlockSpec((gd, gd), const2),
        ],
        out_specs=pl.BlockSpec((1, t2, d), lambda b, k: (b, 0, k)),
        out_shape=jax.ShapeDtypeStruct((bsz, t2, t1 * d), BF16),
        compiler_params=_params(("parallel", "parallel"), 16 * t2 * d * 4),
        name="dft2",
    )(*ys, cs2, cc, sc)
    return out.reshape(bsz, t, d)


def _merge_kernel(x_ref, ha_ref, hb_ref, hc_ref, m0_ref, m1_ref, m2_ref, mod_ref, wb_ref, wo_ref,
                  g_ref, b_ref, o_ref, *, alpha):
    acc = None
    for j, (h_ref, m_ref) in enumerate(((ha_ref, m0_ref), (hb_ref, m1_ref), (hc_ref, m2_ref))):
        term = _sigmoid(m_ref[...].astype(F32)) * _dot(h_ref[...], wb_ref[j])
        acc = term if acc is None else acc + term
    y = _dot(acc.astype(BF16), wo_ref[...])
    z = alpha * x_ref[...] + mod_ref[0, 2:3, :] * y
    o_ref[...] = _layer_norm(z, g_ref[...], b_ref[...])


def _merge(x, h_a, h_b, h_c, p, mcol, mod, wb, wo, g, b, alpha):
    bsz, t, d = x.shape
    n = bsz * t
    tm = min(512, t)
    tpb = t // tm
    row = lambda i: (i, 0)
    p2 = p.reshape(n, p.shape[-1])
    mspec = lambda j: pl.BlockSpec((tm, d), lambda i: (i, mcol + j))
    out = pl.pallas_call(
        functools.partial(_merge_kernel, alpha=alpha),
        grid=(n // tm,),
        in_specs=[
            pl.BlockSpec((tm, d), row), pl.BlockSpec((tm, d), row), pl.BlockSpec((tm, d), row),
            pl.BlockSpec((tm, d), row), mspec(0), mspec(1), mspec(2),
            pl.BlockSpec((1, 3, d), lambda i: (i // tpb, 0, 0)),
            pl.BlockSpec((N_BRANCH, d, d), lambda i: (0, 0, 0)),
            pl.BlockSpec((d, d), lambda i: (0, 0)),
            pl.BlockSpec((1, d), lambda i: (0, 0)),
            pl.BlockSpec((1, d), lambda i: (0, 0)),
        ],
        out_specs=pl.BlockSpec((tm, d), row),
        out_shape=jax.ShapeDtypeStruct((n, d), F32),
        compiler_params=_params(("parallel",), 16 * d * d + 40 * tm * d),
        name="merge",
    )(x.reshape(n, d), h_a.reshape(n, d), h_b.reshape(n, d), h_c.reshape(n, d), p2, p2, p2, mod,
      wb, wo, g.reshape(1, d), b.reshape(1, d))
    return out.reshape(bsz, t, d)


def _mixer_weights(w_in, b_gates, qk_w, d):
    dm = d
    o0 = 3 * dm + N_GATES
    w_qkv = w_in[:, :3 * dm]
    w_g = jnp.pad(w_in[:, 3 * dm:o0], ((0, 0), (0, V7X_LANES - N_GATES)))
    w_o = w_in[:, o0:o0 + dm]
    w_f = w_in[:, o0 + dm:o0 + 2 * dm]
    w_cv = w_in[:, o0 + 2 * dm:o0 + 3 * dm]
    w_cg = w_in[:, o0 + 3 * dm:o0 + 4 * dm]
    w_m = w_in[:, o0 + 4 * dm:]
    bg = jnp.pad(b_gates, (0, V7X_LANES - N_GATES)).reshape(1, V7X_LANES)
    kscale = jnp.concatenate([jnp.ones((dm,), F32), jnp.full((dm,), (dm // HEADS) ** -0.5, F32)])
    return dict(qkv=w_qkv.astype(BF16), g=w_g.astype(BF16), bg=bg, o=w_o.astype(BF16),
                f=w_f.astype(BF16), cv=w_cv.astype(BF16), cg=w_cg.astype(BF16),
                m=w_m.astype(BF16), qk_w=qk_w * kscale[None, :])


def _mixer(x, u, mod, mw, lw, init, alpha, states_only=False):
    bsz, t, d = x.shape
    u2 = u.reshape(bsz * t, d)
    zero = lambda m: jnp.zeros((1, m), F32)
    gates = _linear(u2, mw["g"], mw["bg"], F32).reshape(bsz, t, V7X_LANES)
    if states_only:
        p = _linear(u2, mw["qkv"], zero(3 * d), BF16).reshape(bsz, t, 3 * d)
        qk = _dwconv(p, 2 * d, mw["qk_w"], BF16)
        _, states = _mlstm(qk, p, gates, init, None)
        return None, states
    w_big = jnp.concatenate([mw["qkv"], mw["o"], mw["m"]], axis=1)
    p = _linear(u2, w_big, zero(w_big.shape[1]), BF16).reshape(bsz, t, -1)
    f = _linear(u2, mw["f"], zero(d), BF16).reshape(bsz, t, d)
    glu = _glu(u2, mw["cv"], mw["cg"], F32).reshape(bsz, t, d)
    qk = _dwconv(p, 2 * d, mw["qk_w"], BF16)
    h_a, states = _mlstm(qk, p, gates, init, lw["mh_g"])
    h_b = _fnet(f)
    h_c = _dwconv(glu, d, lw["dw_w"], BF16, norm=(lw["dw_b"], lw["cn_g"], lw["cn_b"]))
    y = _merge(x, h_a, h_b, h_c, p, 4, mod, lw["wb"], lw["wo"], lw["ln_g"], lw["ln_b"], alpha)
    return y, states


def kernel(x, c, ctx, c_ctx, w_ada, b_ada, ln_g, ln_b, ffn_w1, ffn_w3, ffn_w2, w_in, b_gates,
           qk_conv_w, mh_norm_g, dw_w, dw_b, conv_norm_g, conv_norm_b, w_branch, w_out):
    bsz, t, d = x.shape
    depth = w_ada.shape[0]
    alpha = (2 * depth) ** 0.25
    dh = d // HEADS

    c_rows = jnp.zeros((8, d), F32).at[:bsz].set(c).at[bsz].set(c_ctx)
    ada = _ada(c_rows, w_ada, b_ada)
    x = _embed(x)
    xc = ctx
    init = (_zero_state(bsz, dh), _zero_state(bsz, dh))

    for l in range(depth):
        last = l == depth - 1
        mod_x = ada[l, :bsz].reshape(bsz, 3, 3, d)
        mod_c = jnp.broadcast_to(ada[l, bsz].reshape(1, 3, 3, d), (bsz, 3, 3, d))
        mw = _mixer_weights(w_in[l], b_gates[l], qk_conv_w[l], d)
        lw = dict(mh_g=mh_norm_g[l], dw_w=dw_w[l], dw_b=dw_b[l], cn_g=conv_norm_g[l],
                  cn_b=conv_norm_b[l], wb=w_branch[l].astype(BF16), wo=w_out[l].astype(BF16),
                  ln_g=ln_g[l, 1], ln_b=ln_b[l, 1])
        ffn = lambda h, mod, modn, s, emit: _ffn(
            h, mod, modn, ffn_w1[l, s].astype(BF16), ffn_w3[l, s].astype(BF16),
            ffn_w2[l, s].astype(BF16), ln_g[l, 2 * s], ln_b[l, 2 * s], alpha, emit)

        x, ux = ffn(x, mod_x[:, 0], mod_x[:, 1], 0, True)
        xc, uc = ffn(xc, mod_c[:, 0], mod_c[:, 1], 0, True)
        yc, ctx_states = _mixer(xc, uc, mod_c[:, 1], mw, lw, init, alpha, states_only=last)
        if not last:
            xc = yc
        x, _ = _mixer(x, ux, mod_x[:, 1], mw, lw, ctx_states, alpha)
        x, _ = ffn(x, mod_x[:, 2], mod_x[:, 2], 1, False)
        if not last:
            xc, _ = ffn(xc, mod_c[:, 2], mod_c[:, 2], 1, False)
    return x
```

```python
import functools

import numpy as np
import jax
import jax.numpy as jnp
from jax import lax
from jax.experimental import pallas as pl
from jax.experimental.pallas import tpu as pltpu

F32 = jnp.float32
BF16 = jnp.bfloat16

GRID_W = 64
HEADS = 4
N_GATES = 16
FOURIER_GROUPS = 4
N_BRANCH = 3
LN_EPS = 1e-5

V7X_VMEM_BYTES = 64 * 1024 * 1024
V7X_LANES = 128
SUBLANES = 8
V7X_BF16_SUBLANES = 16

MLSTM_CHUNK = 256
HALO = V7X_BF16_SUBLANES
NSTREAM = 2 * HEADS


def _vmem_limit(nbytes):
    return int(min(max(nbytes, 16 * 1024 * 1024), V7X_VMEM_BYTES - 6 * 1024 * 1024))


def _params(semantics, nbytes):
    return pltpu.CompilerParams(dimension_semantics=semantics, vmem_limit_bytes=_vmem_limit(nbytes))


def _dot(a, b):
    return jnp.dot(a, b, preferred_element_type=F32)


def _layer_norm(z, g, b):
    mu = jnp.mean(z, axis=-1, keepdims=True)
    d = z - mu
    var = jnp.mean(d * d, axis=-1, keepdims=True)
    return d * lax.rsqrt(var + LN_EPS) * g + b


def _sigmoid(x):
    return 1.0 / (1.0 + jnp.exp(-x))


def _ada_kernel(c_ref, w_ref, b_ref, o_ref):
    c = c_ref[...]
    a = (c * _sigmoid(c)).astype(BF16)
    o_ref[0] = _dot(a, w_ref[0].astype(BF16)) + b_ref[0]


def _ada(c_rows, w_ada, b_ada):
    depth, d, n = w_ada.shape
    tn = 1024
    return pl.pallas_call(
        _ada_kernel,
        grid=(depth, n // tn),
        in_specs=[
            pl.BlockSpec((8, d), lambda l, j: (0, 0)),
            pl.BlockSpec((1, d, tn), lambda l, j: (l, 0, j)),
            pl.BlockSpec((1, 1, tn), lambda l, j: (l, 0, j)),
        ],
        out_specs=pl.BlockSpec((1, 8, tn), lambda l, j: (l, 0, j)),
        out_shape=jax.ShapeDtypeStruct((depth, 8, n), F32),
        compiler_params=_params(("parallel", "parallel"), 4 * d * tn * 4),
        name="ada",
    )(c_rows, w_ada, b_ada.reshape(depth, 1, n))


def _embed_kernel(x_ref, r_ref, c_ref, o_ref):
    half = r_ref.shape[-1]
    o_ref[0, :, :, :half] = x_ref[0, :, :, :half] + r_ref[...]
    o_ref[0, :, :, half:] = x_ref[0, :, :, half:] + c_ref[...][None]


def _embed(x):
    bsz, t, d = x.shape
    rows = t // GRID_W
    quarter = d // 4
    omega = 1.0 / (10000.0 ** (jnp.arange(quarter, dtype=F32) / quarter))
    r = jnp.arange(rows, dtype=F32)[:, None] * omega
    cl = jnp.arange(GRID_W, dtype=F32)[:, None] * omega
    row_emb = jnp.concatenate([jnp.sin(r), jnp.cos(r)], axis=-1).reshape(rows, 1, d // 2)
    col_emb = jnp.concatenate([jnp.sin(cl), jnp.cos(cl)], axis=-1)
    rb = 8
    out = pl.pallas_call(
        _embed_kernel,
        grid=(bsz, rows // rb),
        in_specs=[
            pl.BlockSpec((1, rb, GRID_W, d), lambda b, i: (b, i, 0, 0)),
            pl.BlockSpec((rb, 1, d // 2), lambda b, i: (i, 0, 0)),
            pl.BlockSpec((GRID_W, d // 2), lambda b, i: (0, 0)),
        ],
        out_specs=pl.BlockSpec((1, rb, GRID_W, d), lambda b, i: (b, i, 0, 0)),
        out_shape=jax.ShapeDtypeStruct((bsz, rows, GRID_W, d), x.dtype),
        compiler_params=_params(("parallel", "parallel"), 6 * rb * GRID_W * d * 4),
        name="embed",
    )(x.reshape(bsz, rows, GRID_W, d), row_emb, col_emb)
    return out.reshape(bsz, t, d)


def _ffn_chunks(f):
    chunks, c0 = [], 0
    while c0 < f:
        cw = min(1024, f - c0)
        chunks.append((c0, cw))
        c0 += cw
    return tuple(chunks)


def _ffn_kernel(h_ref, mod_ref, modn_ref, w1_ref, w3_ref, w2_ref, g_ref, b_ref, o_ref, *u_ref,
                alpha, chunks):
    h = h_ref[...]
    shift, scale, gate = mod_ref[0, 0:1, :], mod_ref[0, 1:2, :], mod_ref[0, 2:3, :]
    u = (h * (1.0 + scale) + shift).astype(BF16)
    acc = None
    for c0, cw in chunks:
        a1 = _dot(u, w1_ref[:, c0:c0 + cw])
        a3 = _dot(u, w3_ref[:, c0:c0 + cw])
        hid = (a1 * _sigmoid(a1) * a3).astype(BF16)
        part = _dot(hid, w2_ref[c0:c0 + cw, :])
        acc = part if acc is None else acc + part
    y = _layer_norm(alpha * h + 0.5 * gate * acc, g_ref[...], b_ref[...])
    o_ref[...] = y
    if u_ref:
        u_ref[0][...] = (y * (1.0 + modn_ref[0, 1:2, :]) + modn_ref[0, 0:1, :]).astype(BF16)


def _ffn(h, mod, modn, w1, w3, w2, g, b, alpha, emit_u):
    bsz, t, d = h.shape
    f = w1.shape[1]
    tm = min(512, t)
    tpb = t // tm
    n = bsz * t
    row = lambda i: (i, 0)
    modmap = lambda i: (i // tpb, 0, 0)
    const = lambda i: (0, 0)
    out_shape = [jax.ShapeDtypeStruct((n, d), F32)]
    out_specs = [pl.BlockSpec((tm, d), row)]
    if emit_u:
        out_shape.append(jax.ShapeDtypeStruct((n, d), BF16))
        out_specs.append(pl.BlockSpec((tm, d), row))
    wbytes = 3 * d * f * 2
    res = pl.pallas_call(
        functools.partial(_ffn_kernel, alpha=alpha, chunks=_ffn_chunks(f)),
        grid=(n // tm,),
        in_specs=[
            pl.BlockSpec((tm, d), row),
            pl.BlockSpec((1, 3, d), modmap),
            pl.BlockSpec((1, 3, d), modmap),
            pl.BlockSpec((d, f), const, pipeline_mode=pl.Buffered(1)),
            pl.BlockSpec((d, f), const, pipeline_mode=pl.Buffered(1)),
            pl.BlockSpec((f, d), const, pipeline_mode=pl.Buffered(1)),
            pl.BlockSpec((1, d), const),
            pl.BlockSpec((1, d), const),
        ],
        out_specs=out_specs,
        out_shape=out_shape,
        compiler_params=_params(("parallel",), wbytes + 6 * tm * d * 4 + 6 * tm * 1024 * 4),
        name="ffn",
    )(h.reshape(n, d), mod, modn, w1, w3, w2, g.reshape(1, d), b.reshape(1, d))
    y = res[0].reshape(bsz, t, d)
    return (y, res[1].reshape(bsz, t, d)) if emit_u else (y, None)


def _linear_kernel(u_ref, w_ref, b_ref, o_ref):
    o_ref[...] = (_dot(u_ref[...], w_ref[...]) + b_ref[...]).astype(o_ref.dtype)


def _glu_kernel(u_ref, wa_ref, wb_ref, o_ref):
    u = u_ref[...]
    o_ref[...] = (_dot(u, wa_ref[...]) * _sigmoid(_dot(u, wb_ref[...]))).astype(o_ref.dtype)


def _linear(u, w, bias, out_dtype):
    n, k = u.shape
    m = w.shape[1]
    tm = min(1024, n)
    tn = min(1024, m)
    return pl.pallas_call(
        _linear_kernel,
        grid=(n // tm, m // tn),
        in_specs=[
            pl.BlockSpec((tm, k), lambda i, j: (i, 0)),
            pl.BlockSpec((k, tn), lambda i, j: (0, j)),
            pl.BlockSpec((1, tn), lambda i, j: (0, j)),
        ],
        out_specs=pl.BlockSpec((tm, tn), lambda i, j: (i, j)),
        out_shape=jax.ShapeDtypeStruct((n, m), out_dtype),
        compiler_params=_params(("parallel", "arbitrary"), 4 * tm * k + 4 * k * tn + 16 * tm * tn),
        name="linear",
    )(u, w, bias)


def _glu(u, wa, wb, out_dtype):
    n, k = u.shape
    m = wa.shape[1]
    tm = min(1024, n)
    tn = min(1024, m)
    return pl.pallas_call(
        _glu_kernel,
        grid=(n // tm, m // tn),
        in_specs=[
            pl.BlockSpec((tm, k), lambda i, j: (i, 0)),
            pl.BlockSpec((k, tn), lambda i, j: (0, j)),
            pl.BlockSpec((k, tn), lambda i, j: (0, j)),
        ],
        out_specs=pl.BlockSpec((tm, tn), lambda i, j: (i, j)),
        out_shape=jax.ShapeDtypeStruct((n, m), out_dtype),
        compiler_params=_params(("parallel", "arbitrary"), 4 * tm * k + 8 * k * tn + 20 * tm * tn),
        name="glu",
    )(u, wa, wb)


def _dwconv_kernel(xp_ref, xm_ref, xn_ref, w_ref, *rest, ksize, rb, lw, norm):
    if norm:
        bias_ref, g_ref, b_ref, o_ref, win_ref, cv_ref = rest
    else:
        o_ref, win_ref = rest
    tt, tc = xm_ref.shape[1], xm_ref.shape[2]
    i = pl.program_id(1)
    last = pl.num_programs(1) - 1
    win_ref[0:HALO, :] = jnp.where(i > 0, xp_ref[0].astype(F32), 0.0)
    win_ref[HALO:HALO + tt, :] = xm_ref[0].astype(F32)
    win_ref[HALO + tt:, :] = jnp.where(i < last, xn_ref[0].astype(F32), 0.0)
    pad = ksize // 2
    groups = {}
    for k in range(ksize):
        a, s = divmod(HALO - pad + k, SUBLANES)
        groups.setdefault(s, []).append((a, k))

    for c0 in range(0, tc, lw):
        cols = slice(c0, c0 + lw)

        def block(ib, carry, cols=cols):
            r0 = pl.multiple_of(ib * rb, rb)
            acc = None
            for s, taps in groups.items():
                nrows = rb + (SUBLANES if s else 0)
                part = None
                for a, k in taps:
                    term = win_ref[pl.ds(r0 + a * SUBLANES, nrows), cols] * w_ref[k:k + 1, cols]
                    part = term if part is None else part + term
                if s:
                    part = part[s:s + rb]
                acc = part if acc is None else acc + part
            if norm:
                cv_ref[pl.ds(r0, rb), cols] = acc
            else:
                o_ref[0, pl.ds(r0, rb), cols] = acc.astype(o_ref.dtype)
            return carry

        lax.fori_loop(0, tt // rb, block, 0)

    if norm:
        z = _layer_norm(cv_ref[...] + bias_ref[...], g_ref[...], b_ref[...])
        o_ref[0] = (z * _sigmoid(z)).astype(o_ref.dtype)


def _dwconv(x, ncols, w, out_dtype, norm=None):
    bsz, t, _ = x.shape
    ksize = w.shape[0]
    assert ksize // 2 <= HALO
    tt = min(512, t)
    tc = ncols if norm is not None else min(512, ncols)
    nh = t // HALO
    per = tt // HALO
    in_specs = [
        pl.BlockSpec((1, HALO, tc), lambda b, i, j: (b, jnp.maximum(i * per - 1, 0), j)),
        pl.BlockSpec((1, tt, tc), lambda b, i, j: (b, i, j)),
        pl.BlockSpec((1, HALO, tc), lambda b, i, j: (b, jnp.minimum((i + 1) * per, nh - 1), j)),
        pl.BlockSpec((ksize, tc), lambda b, i, j: (0, j)),
    ]
    args = [x, x, x, w]
    if norm is not None:
        in_specs += [pl.BlockSpec((1, tc), lambda b, i, j: (0, j))] * 3
        args += [a.reshape(1, ncols) for a in norm]
    scratch = [pltpu.VMEM((tt + 2 * HALO, tc), F32)]
    if norm is not None:
        scratch.append(pltpu.VMEM((tt, tc), F32))
    return pl.pallas_call(
        functools.partial(_dwconv_kernel, ksize=ksize, rb=64, lw=V7X_LANES, norm=norm is not None),
        grid=(bsz, t // tt, ncols // tc),
        in_specs=in_specs,
        out_specs=pl.BlockSpec((1, tt, tc), lambda b, i, j: (b, i, j)),
        out_shape=jax.ShapeDtypeStruct((bsz, t, ncols), out_dtype),
        scratch_shapes=scratch,
        compiler_params=_params(("parallel", "parallel", "parallel"), 12 * (tt + 2 * HALO) * tc * 4),
        name="dwconv",
    )(*args)


def _split3(x):
    hi = x.astype(BF16)
    r1 = x - hi.astype(F32)
    mid = r1.astype(BF16)
    lo = (r1 - mid.astype(F32)).astype(BF16)
    return hi, mid, lo


def _log_sigmoid(x):
    return jnp.minimum(x, 0.0) - jnp.log(1.0 + jnp.exp(-jnp.abs(x)))


def _gate_prep_kernel(g_ref, bc_ref, r_ref, be_ref, *, ln):
    t = g_ref.shape[2]
    s_idx = lax.broadcasted_iota(jnp.int32, (ln, ln), 0)
    t_idx = lax.broadcasted_iota(jnp.int32, (ln, ln), 1)
    prefix = (s_idx <= t_idx).astype(BF16)
    suffix = (s_idx >= t_idx).astype(BF16)
    is_fwd = lax.broadcasted_iota(jnp.int32, (NSTREAM, ln), 0) < HEADS
    for c in range(t // ln):
        cols = slice(c * ln, (c + 1) * ln)
        li = g_ref[0, 0:NSTREAM, cols]
        lf = _log_sigmoid(g_ref[0, NSTREAM:2 * NSTREAM, cols])
        parts = _split3(lf)
        bc = jnp.where(is_fwd, sum(_dot(p, prefix) for p in parts), sum(_dot(p, suffix) for p in parts))
        bc_ref[0, :, cols] = bc
        r_ref[0, :, cols] = li - bc
        be_ref[0, :, cols] = jnp.broadcast_to(jnp.sum(lf, axis=1, keepdims=True), (NSTREAM, ln))


def _gate_prep(gates, ln):
    bsz, t, _ = gates.shape
    g = gates[:, :, :N_GATES].reshape(bsz, t, 2, 2, HEADS).transpose(0, 3, 2, 4, 1)
    g = g.reshape(bsz, 2 * NSTREAM, t)
    blk = pl.BlockSpec((1, NSTREAM, t), lambda b: (b, 0, 0))
    bc, r, be = pl.pallas_call(
        functools.partial(_gate_prep_kernel, ln=ln),
        grid=(bsz,),
        in_specs=[pl.BlockSpec((1, 2 * NSTREAM, t), lambda b: (b, 0, 0))],
        out_specs=[blk, blk, blk],
        out_shape=[jax.ShapeDtypeStruct((bsz, NSTREAM, t), F32)] * 3,
        compiler_params=_params(("parallel",), 16 * 2 * NSTREAM * t * 4),
        name="gate_prep",
    )(g)
    return bc.transpose(0, 2, 1), r.transpose(0, 2, 1), r, be


def _mlstm_kernel(qf_ref, kf_ref, vf_ref, qb_ref, kb_ref, vb_ref, bcf_ref, rcf_ref, bcb_ref, rcb_ref,
                  rrf_ref, bef_ref, rrb_ref, beb_ref, ct0_ref, n0_ref, m0_ref,
                  hf_ref, hb_ref, ctf_ref, nf_ref, mf_ref, ct_s, n_s, m_s):
    c = pl.program_id(1)
    ln = qf_ref.shape[1]
    dh = qf_ref.shape[2] // HEADS

    @pl.when(c == 0)
    def _():
        ct_s[...] = ct0_ref[0]
        n_s[...] = n0_ref[0]
        m_s[...] = m0_ref[0]

    t_idx = lax.broadcasted_iota(jnp.int32, (ln, ln), 0)
    s_idx = lax.broadcasted_iota(jnp.int32, (ln, ln), 1)
    dirs = ((qf_ref, kf_ref, vf_ref, bcf_ref, rcf_ref, rrf_ref, bef_ref, hf_ref),
            (qb_ref, kb_ref, vb_ref, bcb_ref, rcb_ref, rrb_ref, beb_ref, hb_ref))
    for di, (q_ref, k_ref, v_ref, bc_ref, rc_ref, rr_ref, be_ref, h_ref) in enumerate(dirs):
        keep = (s_idx >= t_idx) if di else (s_idx <= t_idx)
        for hd in range(HEADS):
            j = di * HEADS + hd
            cols = slice(hd * dh, (hd + 1) * dh)
            q, k, v = q_ref[0, :, cols], k_ref[0, :, cols], v_ref[0, :, cols]
            b_col, r_col = bc_ref[0, :, j:j + 1], rc_ref[0, :, j:j + 1]
            r_row, b_end = rr_ref[0, j:j + 1, :], be_ref[0, j:j + 1, 0:1]
            m_prev = m_s[j, 0:1, 0:1]

            d_log = jnp.where(keep, b_col + r_row, -jnp.inf)
            m_t = jnp.maximum(b_col + m_prev, jnp.max(d_log, axis=1, keepdims=True))
            scores = lax.dot_general(q, k, (((1,), (1,)), ((), ())), preferred_element_type=F32)
            scores = scores * jnp.exp(d_log - m_t)
            inter_w = jnp.exp(b_col + m_prev - m_t)
            ct = ct_s[j]
            num = _dot(scores.astype(BF16), v) + inter_w * _dot(q, ct.astype(BF16))
            qn = jnp.sum(q.astype(F32) * n_s[j], axis=1, keepdims=True)
            den = jnp.sum(scores, axis=1, keepdims=True) + inter_w * qn
            h_ref[0, :, cols] = num / jnp.maximum(jnp.abs(den), jnp.exp(-m_t))

            m_new = jnp.maximum(b_end + m_prev, b_end + jnp.max(r_row, axis=1, keepdims=True))
            carry = jnp.exp(b_end + m_prev - m_new)
            tok_col = jnp.exp(b_end + r_col - m_new)
            vw = (v.astype(F32) * tok_col).astype(BF16)
            ct_s[j] = carry * ct + lax.dot_general(k, vw, (((0,), (0,)), ((), ())),
                                                   preferred_element_type=F32)
            n_s[j] = carry * n_s[j] + jnp.sum(k.astype(F32) * tok_col, axis=0, keepdims=True)
            m_s[j] = jnp.broadcast_to(m_new, m_s.shape[1:])

    @pl.when(c == pl.num_programs(1) - 1)
    def _():
        ctf_ref[0] = ct_s[...]
        nf_ref[0] = n_s[...]
        mf_ref[0] = m_s[...]


def _zero_state(bsz, dh):
    return (jnp.zeros((bsz, NSTREAM, dh, dh), F32), jnp.zeros((bsz, NSTREAM, 1, dh), F32),
            jnp.zeros((bsz, NSTREAM, SUBLANES, V7X_LANES), F32))


def _mlstm(qk, p, gates, init):
    bsz, t, dm2 = qk.shape
    dm = dm2 // 2
    dh = dm // HEADS
    ln = min(MLSTM_CHUNK, t)
    nc = t // ln
    bc_col, r_col, r_row, be_row = _gate_prep(gates, ln)
    fwd = lambda c: c
    bwd = lambda c: nc - 1 - c
    tok = lambda cm, col: pl.BlockSpec((1, ln, dm), lambda b, c: (b, cm(c), col))
    colv = lambda cm: pl.BlockSpec((1, ln, NSTREAM), lambda b, c: (b, cm(c), 0))
    rowv = lambda cm: pl.BlockSpec((1, NSTREAM, ln), lambda b, c: (b, 0, cm(c)))
    st = lambda s: pl.BlockSpec((1, NSTREAM) + s, lambda b, c: (b, 0, 0, 0))
    st_shapes = ((dh, dh), (1, dh), (SUBLANES, V7X_LANES))
    outs = pl.pallas_call(
        _mlstm_kernel,
        grid=(bsz, nc),
        in_specs=[tok(fwd, 0), tok(fwd, 1), tok(fwd, 2), tok(bwd, 0), tok(bwd, 1), tok(bwd, 2),
                  colv(fwd), colv(fwd), colv(bwd), colv(bwd),
                  rowv(fwd), rowv(fwd), rowv(bwd), rowv(bwd)] + [st(s) for s in st_shapes],
        out_specs=[tok(fwd, 0), tok(bwd, 0)] + [st(s) for s in st_shapes],
        out_shape=[jax.ShapeDtypeStruct((bsz, t, dm), F32)] * 2
        + [jax.ShapeDtypeStruct((bsz, NSTREAM) + s, F32) for s in st_shapes],
        scratch_shapes=[pltpu.VMEM((NSTREAM,) + s, F32) for s in st_shapes],
        compiler_params=_params(("parallel", "arbitrary"),
                                10 * NSTREAM * dh * dh * 4 + 40 * ln * dm * 4),
        name="mlstm",
    )(qk, qk, p, qk, qk, p, bc_col, r_col, bc_col, r_col, r_row, be_row, r_row, be_row, *init)
    return outs[0], outs[1], tuple(outs[2:])


def _dft_mats(n):
    ang = 2.0 * np.pi * np.outer(np.arange(n), np.arange(n)) / n
    return np.cos(ang), np.sin(ang)


def _dft1_kernel(x_ref, k_ref, twc_ref, tws_ref, yr_ref, yi_ref):
    _, t1, nb, d = x_ref.shape
    rows = t1 * nb
    a = _dot(k_ref[...], x_ref[0].reshape(rows, d))
    zr, zi = a[:rows], -a[rows:]
    ct, st = twc_ref[0], tws_ref[0]
    yr_ref[0] = (zr * ct + zi * st).astype(yr_ref.dtype).reshape(t1, nb, d)
    yi_ref[0] = (zi * ct - zr * st).astype(yi_ref.dtype).reshape(t1, nb, d)


def _dft2_kernel(*refs, real_input, groups):
    if real_input:
        yr_ref, cs_ref, cc_ref, sc_ref, o_ref = refs
    else:
        yr_ref, yi_ref, cs_ref, cc_ref, sc_ref, o_ref = refs
    t2 = cs_ref.shape[1]
    cs = cs_ref[...]
    a = _dot(cs, yr_ref[0, 0])
    if real_input:
        vr, vi = a[:t2], -a[t2:]
    else:
        b = _dot(cs, yi_ref[0, 0])
        vr, vi = a[:t2] + b[t2:], b[:t2] - a[t2:]
    gd = vr.shape[1] // groups
    for g in range(groups):
        sl = slice(g * gd, (g + 1) * gd)
        o_ref[0, :, sl] = (_dot(vr[:, sl].astype(BF16), cc_ref[...])
                           + _dot(vi[:, sl].astype(BF16), sc_ref[...])).astype(o_ref.dtype)


def _fnet(f):
    bsz, t, d = f.shape
    gd = d // FOURIER_GROUPS
    t2 = 256 if t >= 2048 else t
    t1 = t // t2
    scale = 1.0 / np.sqrt(float(t) * gd)
    cc, sc = _dft_mats(gd)
    cc = jnp.asarray(cc * scale, BF16)
    sc = jnp.asarray(sc * scale, BF16)
    c2, s2 = _dft_mats(t2)
    cs2 = jnp.asarray(np.concatenate([c2, s2], axis=0), BF16)
    const2 = lambda b, k: (0, 0)
    if t1 == 1:
        ys = [f.reshape(bsz, 1, t2, d)]
    else:
        nb = V7X_BF16_SUBLANES
        rows = t1 * nb
        c1, s1 = _dft_mats(t1)
        eye = np.eye(nb)
        kmat = jnp.asarray(np.concatenate([np.kron(c1, eye), np.kron(s1, eye)], axis=0), BF16)
        ang = 2.0 * np.pi * np.outer(np.arange(t1), np.arange(t2)) / t
        tw = lambda m: jnp.asarray(
            m.reshape(t1, t2 // nb, nb).transpose(1, 0, 2).reshape(t2 // nb, rows, 1), F32)
        blk = pl.BlockSpec((1, t1, nb, d), lambda b, j: (b, 0, j, 0))
        twb = pl.BlockSpec((1, rows, 1), lambda b, j: (j, 0, 0))
        ys = pl.pallas_call(
            _dft1_kernel,
            grid=(bsz, t2 // nb),
            in_specs=[blk, pl.BlockSpec((2 * rows, rows), lambda b, j: (0, 0)), twb, twb],
            out_specs=[blk, blk],
            out_shape=[jax.ShapeDtypeStruct((bsz, t1, t2, d), BF16)] * 2,
            compiler_params=_params(("parallel", "parallel"), 12 * rows * d * 4 + 8 * rows * rows),
            name="dft1",
        )(f.reshape(bsz, t1, t2, d), kmat, tw(np.cos(ang)), tw(np.sin(ang)))
    yblk = pl.BlockSpec((1, 1, t2, d), lambda b, k: (b, k, 0, 0))
    out = pl.pallas_call(
        functools.partial(_dft2_kernel, real_input=t1 == 1, groups=FOURIER_GROUPS),
        grid=(bsz, t1),
        in_specs=[yblk] * len(ys) + [
            pl.BlockSpec((2 * t2, t2), const2),
            pl.BlockSpec((gd, gd), const2),
            pl.BlockSpec((gd, gd), const2),
        ],
        out_specs=pl.BlockSpec((1, t2, d), lambda b, k: (b, 0, k)),
        out_shape=jax.ShapeDtypeStruct((bsz, t2, t1 * d), BF16),
        compiler_params=_params(("parallel", "parallel"), 24 * t2 * d * 4),
        name="dft2",
    )(*ys, cs2, cc, sc)
    return out.reshape(bsz, t, d)


def _merge_kernel(x_ref, hf_ref, hbw_ref, og_ref, mg_ref, hb_ref, hc_ref, m0_ref, m1_ref, m2_ref,
                  mod_ref, wb_ref, wo_ref, g_ref, b_ref, o_ref, *, alpha):
    hs = hf_ref[...] + hbw_ref[...]
    dh = hs.shape[1] // HEADS
    heads = []
    for hd in range(HEADS):
        hh = hs[:, hd * dh:(hd + 1) * dh]
        mu = jnp.mean(hh, axis=-1, keepdims=True)
        dlt = hh - mu
        var = jnp.mean(dlt * dlt, axis=-1, keepdims=True)
        heads.append(dlt * lax.rsqrt(var + LN_EPS))
    h_a = (jnp.concatenate(heads, axis=1) * mg_ref[...] * _sigmoid(og_ref[...].astype(F32))).astype(BF16)
    acc = None
    for j, (h, m_ref) in enumerate(((h_a, m0_ref), (hb_ref[...], m1_ref), (hc_ref[...], m2_ref))):
        term = _sigmoid(m_ref[...].astype(F32)) * _dot(h, wb_ref[j])
        acc = term if acc is None else acc + term
    y = _dot(acc.astype(BF16), wo_ref[...])
    z = alpha * x_ref[...] + mod_ref[0, 2:3, :] * y
    o_ref[...] = _layer_norm(z, g_ref[...], b_ref[...])


def _merge(x, h_f, h_bw, h_b, h_c, p, ocol, mcol, mod, mh_g, wb, wo, g, b, alpha):
    bsz, t, d = x.shape
    n = bsz * t
    tm = min(256, t)
    tpb = t // tm
    row = lambda i: (i, 0)
    const = lambda i: (0, 0)
    p2 = p.reshape(n, p.shape[-1])
    pcol = lambda j: pl.BlockSpec((tm, d), lambda i: (i, j))
    tile = pl.BlockSpec((tm, d), row)
    out = pl.pallas_call(
        functools.partial(_merge_kernel, alpha=alpha),
        grid=(n // tm,),
        in_specs=[
            tile, tile, tile, pcol(ocol), pl.BlockSpec((1, d), const), tile, tile,
            pcol(mcol), pcol(mcol + 1), pcol(mcol + 2),
            pl.BlockSpec((1, 3, d), lambda i: (i // tpb, 0, 0)),
            pl.BlockSpec((N_BRANCH, d, d), lambda i: (0, 0, 0), pipeline_mode=pl.Buffered(1)),
            pl.BlockSpec((d, d), const, pipeline_mode=pl.Buffered(1)),
            pl.BlockSpec((1, d), const),
            pl.BlockSpec((1, d), const),
        ],
        out_specs=tile,
        out_shape=jax.ShapeDtypeStruct((n, d), F32),
        compiler_params=_params(("parallel",), 8 * d * d + 80 * tm * d),
        name="merge",
    )(x.reshape(n, d), h_f.reshape(n, d), h_bw.reshape(n, d), p2, mh_g.reshape(1, d),
      h_b.reshape(n, d), h_c.reshape(n, d), p2, p2, p2, mod, wb, wo, g.reshape(1, d), b.reshape(1, d))
    return out.reshape(bsz, t, d)


def _mixer_weights(w_in, b_gates, qk_w, d):
    dm = d
    o0 = 3 * dm + N_GATES
    w_qkv = w_in[:, :3 * dm]
    w_g = jnp.pad(w_in[:, 3 * dm:o0], ((0, 0), (0, V7X_LANES - N_GATES)))
    w_o = w_in[:, o0:o0 + dm]
    w_f = w_in[:, o0 + dm:o0 + 2 * dm]
    w_cv = w_in[:, o0 + 2 * dm:o0 + 3 * dm]
    w_cg = w_in[:, o0 + 3 * dm:o0 + 4 * dm]
    w_m = w_in[:, o0 + 4 * dm:]
    bg = jnp.pad(b_gates, (0, V7X_LANES - N_GATES)).reshape(1, V7X_LANES)
    kscale = jnp.concatenate([jnp.ones((dm,), F32), jnp.full((dm,), (dm // HEADS) ** -0.5, F32)])
    return dict(qkv=w_qkv.astype(BF16), g=w_g.astype(BF16), bg=bg, o=w_o.astype(BF16),
                f=w_f.astype(BF16), cv=w_cv.astype(BF16), cg=w_cg.astype(BF16),
                m=w_m.astype(BF16), qk_w=qk_w * kscale[None, :])


def _mixer(x, u, mod, mw, lw, init, alpha, states_only=False):
    bsz, t, d = x.shape
    u2 = u.reshape(bsz * t, d)
    zero = lambda m: jnp.zeros((1, m), F32)
    gates = _linear(u2, mw["g"], mw["bg"], F32).reshape(bsz, t, V7X_LANES)
    if states_only:
        p = _linear(u2, mw["qkv"], zero(3 * d), BF16).reshape(bsz, t, 3 * d)
        qk = _dwconv(p, 2 * d, mw["qk_w"], BF16)
        _, _, states = _mlstm(qk, p, gates, init)
        return None, states
    w_big = jnp.concatenate([mw["qkv"], mw["o"], mw["m"]], axis=1)
    p = _linear(u2, w_big, zero(w_big.shape[1]), BF16).reshape(bsz, t, -1)
    f = _linear(u2, mw["f"], zero(d), BF16).reshape(bsz, t, d)
    glu = _glu(u2, mw["cv"], mw["cg"], F32).reshape(bsz, t, d)
    qk = _dwconv(p, 2 * d, mw["qk_w"], BF16)
    h_f, h_bw, states = _mlstm(qk, p, gates, init)
    h_b = _fnet(f)
    h_c = _dwconv(glu, d, lw["dw_w"], BF16, norm=(lw["dw_b"], lw["cn_g"], lw["cn_b"]))
    y = _merge(x, h_f, h_bw, h_b, h_c, p, 3, 4, mod, lw["mh_g"], lw["wb"], lw["wo"],
               lw["ln_g"], lw["ln_b"], alpha)
    return y, states


def kernel(x, c, ctx, c_ctx, w_ada, b_ada, ln_g, ln_b, ffn_w1, ffn_w3, ffn_w2, w_in, b_gates,
           qk_conv_w, mh_norm_g, dw_w, dw_b, conv_norm_g, conv_norm_b, w_branch, w_out):
    bsz, t, d = x.shape
    depth = w_ada.shape[0]
    alpha = (2 * depth) ** 0.25
    dh = d // HEADS

    c_rows = jnp.zeros((8, d), F32).at[:bsz].set(c).at[bsz].set(c_ctx)
    ada = _ada(c_rows, w_ada, b_ada)
    x = _embed(x)
    xc = ctx
    init = _zero_state(bsz, dh)

    for l in range(depth):
        last = l == depth - 1
        mod_x = ada[l, :bsz].reshape(bsz, 3, 3, d)
        mod_c = jnp.broadcast_to(ada[l, bsz].reshape(1, 3, 3, d), (bsz, 3, 3, d))
        mw = _mixer_weights(w_in[l], b_gates[l], qk_conv_w[l], d)
        lw = dict(mh_g=mh_norm_g[l], dw_w=dw_w[l], dw_b=dw_b[l], cn_g=conv_norm_g[l],
                  cn_b=conv_norm_b[l], wb=w_branch[l].astype(BF16), wo=w_out[l].astype(BF16),
                  ln_g=ln_g[l, 1], ln_b=ln_b[l, 1])
        ffn = lambda h, mod, modn, s, emit: _ffn(
            h, mod, modn, ffn_w1[l, s].astype(BF16), ffn_w3[l, s].astype(BF16),
            ffn_w2[l, s].astype(BF16), ln_g[l, 2 * s], ln_b[l, 2 * s], alpha, emit)

        x, ux = ffn(x, mod_x[:, 0], mod_x[:, 1], 0, True)
        xc, uc = ffn(xc, mod_c[:, 0], mod_c[:, 1], 0, True)
        yc, ctx_states = _mixer(xc, uc, mod_c[:, 1], mw, lw, init, alpha, states_only=last)
        if not last:
            xc = yc
        x, _ = _mixer(x, ux, mod_x[:, 1], mw, lw, ctx_states, alpha)
        x, _ = ffn(x, mod_x[:, 2], mod_x[:, 2], 1, False)
        if not last:
            xc, _ = ffn(xc, mod_c[:, 2], mod_c[:, 2], 1, False)
    return x
```

```python
import functools

import numpy as np
import jax
import jax.numpy as jnp
from jax import lax
from jax.experimental import pallas as pl
from jax.experimental.pallas import tpu as pltpu

F32 = jnp.float32
BF16 = jnp.bfloat16

GRID_W = 64
HEADS = 4
N_GATES = 16
FOURIER_GROUPS = 4
N_BRANCH = 3
LN_EPS = 1e-5

V7X_VMEM_BYTES = 64 * 1024 * 1024
V7X_LANES = 128
SUBLANES = 8
V7X_BF16_SUBLANES = 16

MLSTM_CHUNK = 256
HALO = V7X_BF16_SUBLANES
NSTREAM = 2 * HEADS
SCAN_GROUP = 4


def _vmem_limit(nbytes):
    return int(min(max(nbytes, 16 * 1024 * 1024), V7X_VMEM_BYTES - 6 * 1024 * 1024))


def _params(semantics, nbytes, flags=None):
    return pltpu.CompilerParams(dimension_semantics=semantics, vmem_limit_bytes=_vmem_limit(nbytes),
                                flags=flags)


def _dot(a, b):
    return jnp.dot(a, b, preferred_element_type=F32)


def _layer_norm(z, g, b):
    mu = jnp.mean(z, axis=-1, keepdims=True)
    d = z - mu
    var = jnp.mean(d * d, axis=-1, keepdims=True)
    return d * lax.rsqrt(var + LN_EPS) * g + b


def _sigmoid(x):
    return 1.0 / (1.0 + jnp.exp(-x))


def _ada_kernel(c_ref, w_ref, b_ref, o_ref):
    c = c_ref[...]
    a = (c * _sigmoid(c)).astype(BF16)
    o_ref[0] = _dot(a, w_ref[0].astype(BF16)) + b_ref[0]


def _ada(c_rows, w_ada, b_ada):
    depth, d, n = w_ada.shape
    tn = 1024
    return pl.pallas_call(
        _ada_kernel,
        grid=(depth, n // tn),
        in_specs=[
            pl.BlockSpec((8, d), lambda l, j: (0, 0)),
            pl.BlockSpec((1, d, tn), lambda l, j: (l, 0, j)),
            pl.BlockSpec((1, 1, tn), lambda l, j: (l, 0, j)),
        ],
        out_specs=pl.BlockSpec((1, 8, tn), lambda l, j: (l, 0, j)),
        out_shape=jax.ShapeDtypeStruct((depth, 8, n), F32),
        compiler_params=_params(("parallel", "parallel"), 4 * d * tn * 4),
        name="ada",
    )(c_rows, w_ada, b_ada.reshape(depth, 1, n))


def _ffn_chunks(f):
    chunks, c0 = [], 0
    while c0 < f:
        cw = min(1024, f - c0)
        chunks.append((c0, cw))
        c0 += cw
    return tuple(chunks)


def _ffn_kernel(*refs, alpha, chunks, emit_u, embed):
    refs = list(refs)
    h_ref, mod_ref, modn_ref, w1_ref, w3_ref, w2_ref, g_ref, b_ref = refs[:8]
    rest = refs[8:]
    h = h_ref[...]
    if embed:
        r_ref, c_ref = rest[:2]
        rest = rest[2:]
        tm, d = h.shape
        shape = (tm // GRID_W, GRID_W, d // 2)
        emb = jnp.concatenate([jnp.broadcast_to(r_ref[...], shape),
                               jnp.broadcast_to(c_ref[...][None], shape)], axis=-1)
        h = (h.reshape(tm // GRID_W, GRID_W, d) + emb).reshape(tm, d)
    o_ref = rest[0]
    shift, scale, gate = mod_ref[0, 0:1, :], mod_ref[0, 1:2, :], mod_ref[0, 2:3, :]
    u = (h * (1.0 + scale) + shift).astype(BF16)
    acc = None
    for c0, cw in chunks:
        a1 = _dot(u, w1_ref[0, 0, :, c0:c0 + cw])
        a3 = _dot(u, w3_ref[0, 0, :, c0:c0 + cw])
        hid = (a1 * _sigmoid(a1) * a3).astype(BF16)
        part = _dot(hid, w2_ref[0, 0, c0:c0 + cw, :])
        acc = part if acc is None else acc + part
    y = _layer_norm(alpha * h + 0.5 * gate * acc, g_ref[...], b_ref[...])
    o_ref[...] = y
    if emit_u:
        rest[1][...] = (y * (1.0 + modn_ref[0, 1:2, :]) + modn_ref[0, 0:1, :]).astype(BF16)


def _ffn(h, mod, modn, w1, w3, w2, l, s, g, b, alpha, emit_u, embed=False):
    bsz, t, d = h.shape
    f = w1.shape[-1]
    tm = min(512, t)
    tpb = t // tm
    n = bsz * t
    row = lambda i: (i, 0)
    modmap = lambda i: (i // tpb, 0, 0)
    const = lambda i: (0, 0)
    wsel = lambda i: (l, s, 0, 0)
    in_specs = [
        pl.BlockSpec((tm, d), row),
        pl.BlockSpec((1, 3, d), modmap),
        pl.BlockSpec((1, 3, d), modmap),
        pl.BlockSpec((1, 1, d, f), wsel, pipeline_mode=pl.Buffered(1)),
        pl.BlockSpec((1, 1, d, f), wsel, pipeline_mode=pl.Buffered(1)),
        pl.BlockSpec((1, 1, f, d), wsel, pipeline_mode=pl.Buffered(1)),
        pl.BlockSpec((1, d), const),
        pl.BlockSpec((1, d), const),
    ]
    args = [h.reshape(n, d), mod, modn, w1, w3, w2, g.reshape(1, d), b.reshape(1, d)]
    if embed:
        rows = t // GRID_W
        quarter = d // 4
        omega = 1.0 / (10000.0 ** (jnp.arange(quarter, dtype=F32) / quarter))
        r = jnp.arange(rows, dtype=F32)[:, None] * omega
        cl = jnp.arange(GRID_W, dtype=F32)[:, None] * omega
        row_emb = jnp.concatenate([jnp.sin(r), jnp.cos(r)], axis=-1).reshape(rows, 1, d // 2)
        col_emb = jnp.concatenate([jnp.sin(cl), jnp.cos(cl)], axis=-1)
        in_specs += [pl.BlockSpec((tm // GRID_W, 1, d // 2), lambda i: (i % tpb, 0, 0)),
                     pl.BlockSpec((GRID_W, d // 2), const)]
        args += [row_emb, col_emb]
    out_shape = [jax.ShapeDtypeStruct((n, d), F32)]
    out_specs = [pl.BlockSpec((tm, d), row)]
    if emit_u:
        out_shape.append(jax.ShapeDtypeStruct((n, d), BF16))
        out_specs.append(pl.BlockSpec((tm, d), row))
    wbytes = 3 * d * f * 2
    res = pl.pallas_call(
        functools.partial(_ffn_kernel, alpha=alpha, chunks=_ffn_chunks(f), emit_u=emit_u, embed=embed),
        grid=(n // tm,),
        in_specs=in_specs,
        out_specs=out_specs,
        out_shape=out_shape,
        compiler_params=_params(("parallel",), wbytes + 8 * tm * d * 4 + 6 * tm * 1024 * 4),
        name="ffn",
    )(*args)
    y = res[0].reshape(bsz, t, d)
    return (y, res[1].reshape(bsz, t, d)) if emit_u else (y, None)


def _linear_kernel(u_ref, w_ref, b_ref, o_ref):
    o_ref[...] = (_dot(u_ref[...], w_ref[0]) + b_ref[...]).astype(o_ref.dtype)


def _glu_kernel(u_ref, wa_ref, wb_ref, o_ref):
    u = u_ref[...]
    o_ref[...] = (_dot(u, wa_ref[0]) * _sigmoid(_dot(u, wb_ref[0]))).astype(o_ref.dtype)


def _linear(u, w, l, col0, m, bias, out_dtype):
    n, k = u.shape
    tm = min(1024, n)
    tn = min(1024, m)
    cb = col0 // tn
    return pl.pallas_call(
        _linear_kernel,
        grid=(n // tm, m // tn),
        in_specs=[
            pl.BlockSpec((tm, k), lambda i, j: (i, 0)),
            pl.BlockSpec((1, k, tn), lambda i, j: (l, 0, cb + j)),
            pl.BlockSpec((1, tn), lambda i, j: (0, j)),
        ],
        out_specs=pl.BlockSpec((tm, tn), lambda i, j: (i, j)),
        out_shape=jax.ShapeDtypeStruct((n, m), out_dtype),
        compiler_params=_params(("parallel", "arbitrary"), 4 * tm * k + 4 * k * tn + 16 * tm * tn),
        name="linear",
    )(u, w, bias)


def _glu(u, w, l, cola, colb, m, out_dtype):
    n, k = u.shape
    tm = min(1024, n)
    tn = min(1024, m)
    return pl.pallas_call(
        _glu_kernel,
        grid=(n // tm, m // tn),
        in_specs=[
            pl.BlockSpec((tm, k), lambda i, j: (i, 0)),
            pl.BlockSpec((1, k, tn), lambda i, j: (l, 0, cola // tn + j)),
            pl.BlockSpec((1, k, tn), lambda i, j: (l, 0, colb // tn + j)),
        ],
        out_specs=pl.BlockSpec((tm, tn), lambda i, j: (i, j)),
        out_shape=jax.ShapeDtypeStruct((n, m), out_dtype),
        compiler_params=_params(("parallel", "arbitrary"), 4 * tm * k + 8 * k * tn + 20 * tm * tn),
        name="glu",
    )(u, w, w)


def _dwconv_kernel(xp_ref, xm_ref, xn_ref, w_ref, *rest, ksize, rb, lw, norm):
    if norm:
        bias_ref, g_ref, b_ref, o_ref, win_ref, cv_ref = rest
    else:
        o_ref, win_ref = rest
    tt, tc = xm_ref.shape[1], xm_ref.shape[2]
    i = pl.program_id(1)
    last = pl.num_programs(1) - 1
    win_ref[0:HALO, :] = jnp.where(i > 0, xp_ref[0].astype(F32), 0.0)
    win_ref[HALO:HALO + tt, :] = xm_ref[0].astype(F32)
    win_ref[HALO + tt:, :] = jnp.where(i < last, xn_ref[0].astype(F32), 0.0)
    pad = ksize // 2
    groups = {}
    for k in range(ksize):
        a, s = divmod(HALO - pad + k, SUBLANES)
        groups.setdefault(s, []).append((a, k))

    for c0 in range(0, tc, lw):
        cols = slice(c0, c0 + lw)

        def block(ib, carry, cols=cols):
            r0 = pl.multiple_of(ib * rb, rb)
            acc = None
            for s, taps in groups.items():
                nrows = rb + (SUBLANES if s else 0)
                part = None
                for a, k in taps:
                    term = win_ref[pl.ds(r0 + a * SUBLANES, nrows), cols] * w_ref[k:k + 1, cols]
                    part = term if part is None else part + term
                if s:
                    part = part[s:s + rb]
                acc = part if acc is None else acc + part
            if norm:
                cv_ref[pl.ds(r0, rb), cols] = acc
            else:
                o_ref[0, pl.ds(r0, rb), cols] = acc.astype(o_ref.dtype)
            return carry

        lax.fori_loop(0, tt // rb, block, 0)

    if norm:
        z = _layer_norm(cv_ref[...] + bias_ref[...], g_ref[...], b_ref[...])
        o_ref[0] = (z * _sigmoid(z)).astype(o_ref.dtype)


def _dwconv(x, ncols, w, out_dtype, norm=None):
    bsz, t, _ = x.shape
    ksize = w.shape[0]
    assert ksize // 2 <= HALO
    tt = min(512, t)
    tc = ncols if norm is not None else min(512, ncols)
    nh = t // HALO
    per = tt // HALO
    in_specs = [
        pl.BlockSpec((1, HALO, tc), lambda b, i, j: (b, jnp.maximum(i * per - 1, 0), j)),
        pl.BlockSpec((1, tt, tc), lambda b, i, j: (b, i, j)),
        pl.BlockSpec((1, HALO, tc), lambda b, i, j: (b, jnp.minimum((i + 1) * per, nh - 1), j)),
        pl.BlockSpec((ksize, tc), lambda b, i, j: (0, j)),
    ]
    args = [x, x, x, w]
    if norm is not None:
        in_specs += [pl.BlockSpec((1, tc), lambda b, i, j: (0, j))] * 3
        args += [a.reshape(1, ncols) for a in norm]
    scratch = [pltpu.VMEM((tt + 2 * HALO, tc), F32)]
    if norm is not None:
        scratch.append(pltpu.VMEM((tt, tc), F32))
    return pl.pallas_call(
        functools.partial(_dwconv_kernel, ksize=ksize, rb=128, lw=V7X_LANES, norm=norm is not None),
        grid=(bsz, t // tt, ncols // tc),
        in_specs=in_specs,
        out_specs=pl.BlockSpec((1, tt, tc), lambda b, i, j: (b, i, j)),
        out_shape=jax.ShapeDtypeStruct((bsz, t, ncols), out_dtype),
        scratch_shapes=scratch,
        compiler_params=_params(("parallel", "parallel", "parallel"), 12 * (tt + 2 * HALO) * tc * 4),
        name="dwconv",
    )(*args)


def _split3(x):
    hi = x.astype(BF16)
    r1 = x - hi.astype(F32)
    mid = r1.astype(BF16)
    lo = (r1 - mid.astype(F32)).astype(BF16)
    return hi, mid, lo


def _log_sigmoid(x):
    return jnp.minimum(x, 0.0) - jnp.log(1.0 + jnp.exp(-jnp.abs(x)))


def _gate_prep_kernel(g_ref, bc_ref, r_ref, be_ref, *, ln):
    t = g_ref.shape[2]
    s_idx = lax.broadcasted_iota(jnp.int32, (ln, ln), 0)
    t_idx = lax.broadcasted_iota(jnp.int32, (ln, ln), 1)
    prefix = (s_idx <= t_idx).astype(BF16)
    suffix = (s_idx >= t_idx).astype(BF16)
    is_fwd = lax.broadcasted_iota(jnp.int32, (NSTREAM, ln), 0) < HEADS
    for c in range(t // ln):
        cols = slice(c * ln, (c + 1) * ln)
        li = g_ref[0, 0:NSTREAM, cols]
        lf = _log_sigmoid(g_ref[0, NSTREAM:2 * NSTREAM, cols])
        parts = _split3(lf)
        bc = jnp.where(is_fwd, sum(_dot(p, prefix) for p in parts), sum(_dot(p, suffix) for p in parts))
        bc_ref[0, :, cols] = bc
        r_ref[0, :, cols] = li - bc
        be_ref[0, :, cols] = jnp.broadcast_to(jnp.sum(lf, axis=1, keepdims=True), (NSTREAM, ln))


def _gate_prep(gates, ln):
    bsz, t, _ = gates.shape
    g = gates[:, :, :N_GATES].reshape(bsz, t, 2, 2, HEADS).transpose(0, 3, 2, 4, 1)
    g = g.reshape(bsz, 2 * NSTREAM, t)
    blk = pl.BlockSpec((1, NSTREAM, t), lambda b: (b, 0, 0))
    bc, r, be = pl.pallas_call(
        functools.partial(_gate_prep_kernel, ln=ln),
        grid=(bsz,),
        in_specs=[pl.BlockSpec((1, 2 * NSTREAM, t), lambda b: (b, 0, 0))],
        out_specs=[blk, blk, blk],
        out_shape=[jax.ShapeDtypeStruct((bsz, NSTREAM, t), F32)] * 3,
        compiler_params=_params(("parallel",), 16 * 2 * NSTREAM * t * 4),
        name="gate_prep",
    )(g)
    return bc.transpose(0, 2, 1), r.transpose(0, 2, 1), r, be


def _mlstm_kernel(qf_ref, kf_ref, vf_ref, qb_ref, kb_ref, vb_ref, bcf_ref, rcf_ref, bcb_ref, rcb_ref,
                  rrf_ref, bef_ref, rrb_ref, beb_ref, ct0_ref, m0_ref,
                  hf_ref, hb_ref, ctf_ref, mf_ref, ct_s, m_s):
    c = pl.program_id(1)
    ln = qf_ref.shape[1]
    dh = qf_ref.shape[2] // HEADS

    @pl.when(c == 0)
    def _():
        ct_s[...] = ct0_ref[0]
        m_s[...] = m0_ref[0]

    t_idx = lax.broadcasted_iota(jnp.int32, (ln, ln), 0)
    s_idx = lax.broadcasted_iota(jnp.int32, (ln, ln), 1)
    ones_col = (lax.broadcasted_iota(jnp.int32, (ln, V7X_LANES), 1) == 0).astype(BF16)
    dirs = ((qf_ref, kf_ref, vf_ref, bcf_ref, rcf_ref, rrf_ref, bef_ref, hf_ref),
            (qb_ref, kb_ref, vb_ref, bcb_ref, rcb_ref, rrb_ref, beb_ref, hb_ref))
    keeps = (s_idx <= t_idx, s_idx >= t_idx)
    streams = [(di, hd) for di in range(2) for hd in range(HEADS)]
    wide = lambda x: jnp.broadcast_to(x, (ln, V7X_LANES))
    tile = lambda x, n: jnp.concatenate([x] * n, axis=1)

    def scan_group(group):
        st = []
        for di, hd in group:
            q_ref, k_ref, v_ref, bc_ref, rc_ref, rr_ref, be_ref, h_ref = dirs[di]
            j = di * HEADS + hd
            cols = slice(hd * dh, (hd + 1) * dh)
            s = dict(j=j, cols=cols, h_ref=h_ref, q=q_ref[0, :, cols], k=k_ref[0, :, cols],
                     v=jnp.concatenate([v_ref[0, :, cols], ones_col], axis=1),
                     b_col=bc_ref[0, :, j:j + 1], r_col=rc_ref[0, :, j:j + 1],
                     r_row=rr_ref[0, j:j + 1, :], b_end=be_ref[0, j:j + 1, 0:1],
                     m_prev=m_s[j, 0:1, :], ct=ct_s[j])
            s["r_kept"] = jnp.where(keeps[di], s["r_row"], -jnp.inf)
            s["scores"] = lax.dot_general(s["q"], s["k"], (((1,), (1,)), ((), ())),
                                          preferred_element_type=F32)
            s["inter"] = _dot(s["q"], s["ct"].astype(BF16))
            st.append(s)
        for s in st:
            s["a_t"] = jnp.maximum(wide(jnp.max(s["r_kept"], axis=1, keepdims=True)), s["m_prev"])
            s["m_new"] = s["b_end"] + jnp.maximum(s["m_prev"],
                                                  jnp.max(s["r_row"], axis=1, keepdims=True))
        for s in st:
            w = (s["scores"] * jnp.exp(s["r_kept"] - tile(s["a_t"], ln // V7X_LANES))).astype(BF16)
            inter_w = tile(jnp.exp(s["m_prev"] - s["a_t"]), s["inter"].shape[1] // V7X_LANES)
            s["num"] = _dot(w, s["v"]) + inter_w * s["inter"]
        for s in st:
            tok_col = jnp.exp(s["b_end"] + s["r_col"] - s["m_new"][:, 0:1])
            vw = (s["v"].astype(F32) * tok_col).astype(BF16)
            s["upd"] = lax.dot_general(s["k"], vw, (((0,), (0,)), ((), ())),
                                       preferred_element_type=F32)
        for s in st:
            num = s["num"]
            den = jnp.maximum(jnp.abs(wide(num[:, dh:dh + 1])),
                              jnp.exp(-(wide(s["b_col"]) + s["a_t"])))
            s["h_ref"][0, :, s["cols"]] = (num[:, :dh] / tile(den, dh // V7X_LANES)).astype(
                s["h_ref"].dtype)
        for s in st:
            j = s["j"]
            carry = jnp.exp(s["b_end"] + s["m_prev"] - s["m_new"])
            ct_s[j] = tile(carry, s["ct"].shape[1] // V7X_LANES) * s["ct"] + s["upd"]
            m_s[j] = jnp.broadcast_to(s["m_new"], m_s.shape[1:])

    for g0 in range(0, NSTREAM, SCAN_GROUP):
        scan_group(streams[g0:g0 + SCAN_GROUP])

    @pl.when(c == pl.num_programs(1) - 1)
    def _():
        ctf_ref[0] = ct_s[...]
        mf_ref[0] = m_s[...]


def _zero_state(bsz, dh):
    return (jnp.zeros((bsz, NSTREAM, dh, dh + V7X_LANES), F32),
            jnp.zeros((bsz, NSTREAM, SUBLANES, V7X_LANES), F32))


def _mlstm(qk, p, gates, init):
    bsz, t, dm2 = qk.shape
    dm = dm2 // 2
    dh = dm // HEADS
    ln = min(MLSTM_CHUNK, t)
    nc = t // ln
    bc_col, r_col, r_row, be_row = _gate_prep(gates, ln)
    fwd = lambda c: c
    bwd = lambda c: nc - 1 - c
    tok = lambda cm, col: pl.BlockSpec((1, ln, dm), lambda b, c: (b, cm(c), col))
    colv = lambda cm: pl.BlockSpec((1, ln, NSTREAM), lambda b, c: (b, cm(c), 0))
    rowv = lambda cm: pl.BlockSpec((1, NSTREAM, ln), lambda b, c: (b, 0, cm(c)))
    st = lambda s: pl.BlockSpec((1, NSTREAM) + s, lambda b, c: (b, 0, 0, 0))
    st_shapes = ((dh, dh + V7X_LANES), (SUBLANES, V7X_LANES))
    outs = pl.pallas_call(
        _mlstm_kernel,
        grid=(bsz, nc),
        in_specs=[tok(fwd, 0), tok(fwd, 1), tok(fwd, 2), tok(bwd, 0), tok(bwd, 1), tok(bwd, 2),
                  colv(fwd), colv(fwd), colv(bwd), colv(bwd),
                  rowv(fwd), rowv(fwd), rowv(bwd), rowv(bwd)] + [st(s) for s in st_shapes],
        out_specs=[tok(fwd, 0), tok(bwd, 0)] + [st(s) for s in st_shapes],
        out_shape=[jax.ShapeDtypeStruct((bsz, t, dm), BF16)] * 2
        + [jax.ShapeDtypeStruct((bsz, NSTREAM) + s, F32) for s in st_shapes],
        scratch_shapes=[pltpu.VMEM((NSTREAM,) + s, F32) for s in st_shapes],
        compiler_params=_params(("parallel", "arbitrary"),
                                10 * NSTREAM * dh * (dh + V7X_LANES) * 4 + 60 * ln * dm * 4),
        name="mlstm",
    )(qk, qk, p, qk, qk, p, bc_col, r_col, bc_col, r_col, r_row, be_row, r_row, be_row, *init)
    return outs[0], outs[1], tuple(outs[2:])


def _dft_mats(n):
    ang = 2.0 * np.pi * np.outer(np.arange(n), np.arange(n)) / n
    return np.cos(ang), np.sin(ang)


def _dft1_kernel(x_ref, k_ref, twc_ref, tws_ref, yr_ref, yi_ref):
    _, t1, nb, d = x_ref.shape
    rows = t1 * nb
    a = _dot(k_ref[...], x_ref[0].reshape(rows, d))
    zr, zi = a[:rows], -a[rows:]
    ct, st = twc_ref[0], tws_ref[0]
    yr_ref[0] = (zr * ct + zi * st).astype(yr_ref.dtype).reshape(t1, nb, d)
    yi_ref[0] = (zi * ct - zr * st).astype(yi_ref.dtype).reshape(t1, nb, d)


def _dft2_kernel(*refs, real_input, groups):
    if real_input:
        yr_ref, cs_ref, cc_ref, sc_ref, o_ref = refs
    else:
        yr_ref, yi_ref, cs_ref, cc_ref, sc_ref, o_ref = refs
    t2 = cs_ref.shape[1]
    cs = cs_ref[...]
    a = _dot(cs, yr_ref[0, 0])
    if real_input:
        vr, vi = a[:t2], -a[t2:]
    else:
        b = _dot(cs, yi_ref[0, 0])
        vr, vi = a[:t2] + b[t2:], b[:t2] - a[t2:]
    gd = vr.shape[1] // groups
    for g in range(groups):
        sl = slice(g * gd, (g + 1) * gd)
        o_ref[0, :, sl] = (_dot(vr[:, sl].astype(BF16), cc_ref[...])
                           + _dot(vi[:, sl].astype(BF16), sc_ref[...])).astype(o_ref.dtype)


def _fnet(f):
    bsz, t, d = f.shape
    gd = d // FOURIER_GROUPS
    t2 = 256 if t >= 2048 else t
    t1 = t // t2
    scale = 1.0 / np.sqrt(float(t) * gd)
    cc, sc = _dft_mats(gd)
    cc = jnp.asarray(cc * scale, BF16)
    sc = jnp.asarray(sc * scale, BF16)
    c2, s2 = _dft_mats(t2)
    cs2 = jnp.asarray(np.concatenate([c2, s2], axis=0), BF16)
    const2 = lambda b, k: (0, 0)
    if t1 == 1:
        ys = [f.reshape(bsz, 1, t2, d)]
    else:
        nb = V7X_BF16_SUBLANES
        rows = t1 * nb
        c1, s1 = _dft_mats(t1)
        eye = np.eye(nb)
        kmat = jnp.asarray(np.concatenate([np.kron(c1, eye), np.kron(s1, eye)], axis=0), BF16)
        ang = 2.0 * np.pi * np.outer(np.arange(t1), np.arange(t2)) / t
        tw = lambda m: jnp.asarray(
            m.reshape(t1, t2 // nb, nb).transpose(1, 0, 2).reshape(t2 // nb, rows, 1), F32)
        blk = pl.BlockSpec((1, t1, nb, d), lambda b, j: (b, 0, j, 0))
        twb = pl.BlockSpec((1, rows, 1), lambda b, j: (j, 0, 0))
        ys = pl.pallas_call(
            _dft1_kernel,
            grid=(bsz, t2 // nb),
            in_specs=[blk, pl.BlockSpec((2 * rows, rows), lambda b, j: (0, 0)), twb, twb],
            out_specs=[blk, blk],
            out_shape=[jax.ShapeDtypeStruct((bsz, t1, t2, d), BF16)] * 2,
            compiler_params=_params(("parallel", "parallel"), 12 * rows * d * 4 + 8 * rows * rows),
            name="dft1",
        )(f.reshape(bsz, t1, t2, d), kmat, tw(np.cos(ang)), tw(np.sin(ang)))
    yblk = pl.BlockSpec((1, 1, t2, d), lambda b, k: (b, k, 0, 0))
    out = pl.pallas_call(
        functools.partial(_dft2_kernel, real_input=t1 == 1, groups=FOURIER_GROUPS),
        grid=(bsz, t1),
        in_specs=[yblk] * len(ys) + [
            pl.BlockSpec((2 * t2, t2), const2),
            pl.BlockSpec((gd, gd), const2),
            pl.BlockSpec((gd, gd), const2),
        ],
        out_specs=pl.BlockSpec((1, t2, d), lambda b, k: (b, 0, k)),
        out_shape=jax.ShapeDtypeStruct((bsz, t2, t1 * d), BF16),
        compiler_params=_params(("parallel", "parallel"), 24 * t2 * d * 4),
        name="dft2",
    )(*ys, cs2, cc, sc)
    return out.reshape(bsz, t, d)


def _merge_kernel(x_ref, hf_ref, hbw_ref, og_ref, mg_ref, hb_ref, hc_ref, m0_ref, m1_ref, m2_ref,
                  mod_ref, wb_ref, wo_ref, g_ref, b_ref, o_ref, *, alpha):
    hs = hf_ref[...].astype(F32) + hbw_ref[...].astype(F32)
    dh = hs.shape[1] // HEADS
    heads = []
    for hd in range(HEADS):
        hh = hs[:, hd * dh:(hd + 1) * dh]
        mu = jnp.mean(hh, axis=-1, keepdims=True)
        dlt = hh - mu
        var = jnp.mean(dlt * dlt, axis=-1, keepdims=True)
        heads.append(dlt * lax.rsqrt(var + LN_EPS))
    h_a = (jnp.concatenate(heads, axis=1) * mg_ref[...] * _sigmoid(og_ref[...].astype(F32))).astype(BF16)
    acc = None
    for j, (h, m_ref) in enumerate(((h_a, m0_ref), (hb_ref[...], m1_ref), (hc_ref[...], m2_ref))):
        term = _sigmoid(m_ref[...].astype(F32)) * _dot(h, wb_ref[0, j])
        acc = term if acc is None else acc + term
    y = _dot(acc.astype(BF16), wo_ref[0])
    z = alpha * x_ref[...] + mod_ref[0, 2:3, :] * y
    o_ref[...] = _layer_norm(z, g_ref[...], b_ref[...])


def _merge(x, h_f, h_bw, h_b, h_c, p, ocol, mcol, mod, mh_g, wb, wo, l, g, b, alpha):
    bsz, t, d = x.shape
    n = bsz * t
    tm = min(512, t)
    tpb = t // tm
    row = lambda i: (i, 0)
    const = lambda i: (0, 0)
    p2 = p.reshape(n, p.shape[-1])
    pcol = lambda j: pl.BlockSpec((tm, d), lambda i: (i, j))
    tile = pl.BlockSpec((tm, d), row)
    out = pl.pallas_call(
        functools.partial(_merge_kernel, alpha=alpha),
        grid=(n // tm,),
        in_specs=[
            tile, tile, tile, pcol(ocol), pl.BlockSpec((1, d), const), tile, tile,
            pcol(mcol), pcol(mcol + 1), pcol(mcol + 2),
            pl.BlockSpec((1, 3, d), lambda i: (i // tpb, 0, 0)),
            pl.BlockSpec((1, N_BRANCH, d, d), lambda i: (l, 0, 0, 0), pipeline_mode=pl.Buffered(1)),
            pl.BlockSpec((1, d, d), lambda i: (l, 0, 0), pipeline_mode=pl.Buffered(1)),
            pl.BlockSpec((1, d), const),
            pl.BlockSpec((1, d), const),
        ],
        out_specs=tile,
        out_shape=jax.ShapeDtypeStruct((n, d), F32),
        compiler_params=_params(("parallel",), 8 * d * d + 80 * tm * d),
        name="merge",
    )(x.reshape(n, d), h_f.reshape(n, d), h_bw.reshape(n, d), p2, mh_g.reshape(1, d),
      h_b.reshape(n, d), h_c.reshape(n, d), p2, p2, p2, mod, wb, wo, g.reshape(1, d), b.reshape(1, d))
    return out.reshape(bsz, t, d)


def _projection_weights(w_in, d):
    o0 = 3 * d + N_GATES
    qkv, gts = w_in[:, :, :3 * d], w_in[:, :, 3 * d:o0]
    o, rest, m = w_in[:, :, o0:o0 + d], w_in[:, :, o0 + d:o0 + 4 * d], w_in[:, :, o0 + 4 * d:]
    gts = jnp.pad(gts, ((0, 0), (0, 0), (0, V7X_LANES - N_GATES)))
    return jnp.concatenate([qkv, o, m, rest, gts], axis=-1).astype(BF16)


def _mixer(x, u, mod, w_all, l, lw, init, alpha, states_only=False):
    bsz, t, d = x.shape
    u2 = u.reshape(bsz * t, d)
    zero = lambda m: jnp.zeros((1, m), F32)
    nbig = 4 * d + N_BRANCH * d
    gates = _linear(u2, w_all, l, nbig + 3 * d, V7X_LANES, lw["bg"], F32).reshape(bsz, t, V7X_LANES)
    if states_only:
        p = _linear(u2, w_all, l, 0, 3 * d, zero(3 * d), BF16).reshape(bsz, t, 3 * d)
        qk = _dwconv(p, 2 * d, lw["qk_w"], BF16)
        _, _, states = _mlstm(qk, p, gates, init)
        return None, states
    p = _linear(u2, w_all, l, 0, nbig, zero(nbig), BF16).reshape(bsz, t, nbig)
    f = _linear(u2, w_all, l, nbig, d, zero(d), BF16).reshape(bsz, t, d)
    glu = _glu(u2, w_all, l, nbig + d, nbig + 2 * d, d, F32).reshape(bsz, t, d)
    qk = _dwconv(p, 2 * d, lw["qk_w"], BF16)
    h_f, h_bw, states = _mlstm(qk, p, gates, init)
    h_b = _fnet(f)
    h_c = _dwconv(glu, d, lw["dw_w"], BF16, norm=(lw["dw_b"], lw["cn_g"], lw["cn_b"]))
    y = _merge(x, h_f, h_bw, h_b, h_c, p, 3, 4, mod, lw["mh_g"], lw["wb"], lw["wo"], l,
               lw["ln_g"], lw["ln_b"], alpha)
    return y, states


def kernel(x, c, ctx, c_ctx, w_ada, b_ada, ln_g, ln_b, ffn_w1, ffn_w3, ffn_w2, w_in, b_gates,
           qk_conv_w, mh_norm_g, dw_w, dw_b, conv_norm_g, conv_norm_b, w_branch, w_out):
    bsz, t, d = x.shape
    depth = w_ada.shape[0]
    alpha = (2 * depth) ** 0.25
    dh = d // HEADS

    c_rows = jnp.zeros((8, d), F32).at[:bsz].set(c).at[bsz].set(c_ctx)
    ada = _ada(c_rows, w_ada, b_ada)
    xc = ctx
    init = _zero_state(bsz, dh)
    w1, w3, w2 = ffn_w1.astype(BF16), ffn_w3.astype(BF16), ffn_w2.astype(BF16)
    w_all = _projection_weights(w_in, d)
    wb, wo = w_branch.astype(BF16), w_out.astype(BF16)
    kscale = jnp.concatenate([jnp.ones((d,), F32), jnp.full((d,), dh ** -0.5, F32)])

    for l in range(depth):
        last = l == depth - 1
        mod_x = ada[l, :bsz].reshape(bsz, 3, 3, d)
        mod_c = jnp.broadcast_to(ada[l, bsz].reshape(1, 3, 3, d), (bsz, 3, 3, d))
        lw = dict(mh_g=mh_norm_g[l], dw_w=dw_w[l], dw_b=dw_b[l], cn_g=conv_norm_g[l],
                  cn_b=conv_norm_b[l], wb=wb, wo=wo, ln_g=ln_g[l, 1], ln_b=ln_b[l, 1],
                  qk_w=qk_conv_w[l] * kscale[None, :],
                  bg=jnp.pad(b_gates[l], (0, V7X_LANES - N_GATES)).reshape(1, V7X_LANES))
        ffn = lambda h, mod, modn, s, emit, embed=False: _ffn(
            h, mod, modn, w1, w3, w2, l, s, ln_g[l, 2 * s], ln_b[l, 2 * s], alpha, emit, embed)

        x, ux = ffn(x, mod_x[:, 0], mod_x[:, 1], 0, True, embed=l == 0)
        xc, uc = ffn(xc, mod_c[:, 0], mod_c[:, 1], 0, True)
        yc, ctx_states = _mixer(xc, uc, mod_c[:, 1], w_all, l, lw, init, alpha, states_only=last)
        if not last:
            xc = yc
        x, _ = _mixer(x, ux, mod_x[:, 1], w_all, l, lw, ctx_states, alpha)
        x, _ = ffn(x, mod_x[:, 2], mod_x[:, 2], 1, False)
        if not last:
            xc, _ = ffn(xc, mod_c[:, 2], mod_c[:, 2], 1, False)
    return x
```
